```python
import math
import jax, jax.numpy as jnp
from jax import lax
import numpy as np

D_MODEL = 4096
BATCH = 4
SEQ = 4096
DEPTH = 2

GRID_W = 64
HEAD_DIM = 128
ROPE_THETA = 10000.0
Q_BLOCK = 128
LN_EPS = 1e-5
RMS_EPS = 1e-6

NA_HEADS = 8
NA_WIN_ROWS = 8
NA_WIN_COLS = 16
WIDTH_A = NA_HEADS * HEAD_DIM

MLA_HEADS = 8
MLA_Q_RANK = 1536
MLA_KV_RANK = 512
MLA_NOPE = 128
MLA_ROPE = 64
MLA_V = 128
WIDTH_B = MLA_HEADS * MLA_V

DIFF_HEADS = 8
DIFF_QK = 64
DIFF_V = 2 * DIFF_QK
WIDTH_C = DIFF_HEADS * DIFF_V

N_BRANCH = 3
DEEPNORM_ALPHA = (2.0 * DEPTH) ** 0.25
DEEPNORM_BETA = (8.0 * DEPTH) ** -0.25

IN_SPLITS = (
    WIDTH_A, WIDTH_A, WIDTH_A, WIDTH_A,
    MLA_Q_RANK, MLA_KV_RANK, MLA_ROPE, WIDTH_B,
    DIFF_HEADS * 2 * DIFF_QK, DIFF_HEADS * 2 * DIFF_QK, WIDTH_C, WIDTH_C,
    N_BRANCH * D_MODEL,
)
IN_WIDTH = sum(IN_SPLITS)

kernel_name = "hybrid_natten_mla_diffattn_encoder"


def _layer_norm(x, g, b):
    xf = x.astype(jnp.float32)
    mu = jnp.mean(xf, axis=-1, keepdims=True)
    var = jnp.mean(jnp.square(xf - mu), axis=-1, keepdims=True)
    return ((xf - mu) * lax.rsqrt(var + LN_EPS) * g + b).astype(x.dtype)


def _rms_norm(x, g):
    xf = x.astype(jnp.float32)
    return (xf * lax.rsqrt(jnp.mean(xf * xf, axis=-1, keepdims=True) + RMS_EPS) * g).astype(x.dtype)


def _rope(x, pos):
    d = x.shape[-1]
    half = d // 2
    inv_freq = ROPE_THETA ** (-jnp.arange(half, dtype=jnp.float32) * 2.0 / d)
    ang = pos.astype(jnp.float32)[:, None] * inv_freq[None, :]
    cos = jnp.cos(ang)[None, :, None, :]
    sin = jnp.sin(ang)[None, :, None, :]
    x1 = x[..., :half].astype(jnp.float32)
    x2 = x[..., half:].astype(jnp.float32)
    return jnp.concatenate([x1 * cos - x2 * sin, x2 * cos + x1 * sin], axis=-1).astype(x.dtype)


def _split_cols(h):
    points = np.cumsum(np.array(IN_SPLITS))[:-1].tolist()
    return jnp.split(h, points, axis=-1)


def _to_blocks(t):
    b, s = t.shape[0], t.shape[1]
    t = t.reshape((b, s // Q_BLOCK, Q_BLOCK) + t.shape[2:])
    return jnp.moveaxis(t, 1, 0)


def _from_blocks(t):
    t = jnp.moveaxis(t, 0, 1)
    return t.reshape((t.shape[0], t.shape[1] * t.shape[2]) + t.shape[3:])


def _dense_attention(q, k, v, scale):
    def one_block(qb):
        sc = jnp.einsum('bqhd,bkhd->bhqk', qb, k, preferred_element_type=jnp.float32) * scale
        p = jax.nn.softmax(sc, axis=-1)
        return jnp.einsum('bhqk,bkhd->bqhd', p.astype(v.dtype), v)
    return _from_blocks(lax.map(one_block, _to_blocks(q)))


def _diff_attention(q1, k1, q2, k2, v, lam, scale):
    def one_block(qs):
        q1b, q2b = qs
        p1 = jax.nn.softmax(jnp.einsum('bqhd,bkhd->bhqk', q1b, k1, preferred_element_type=jnp.float32) * scale, axis=-1)
        p2 = jax.nn.softmax(jnp.einsum('bqhd,bkhd->bhqk', q2b, k2, preferred_element_type=jnp.float32) * scale, axis=-1)
        p = (p1 - lam * p2).astype(v.dtype)
        return jnp.einsum('bhqk,bkhd->bqhd', p, v)
    return _from_blocks(lax.map(one_block, (_to_blocks(q1), _to_blocks(q2))))


def _neighbourhood_attention(q, k, v, rpb):
    b, s, h, d = q.shape
    rows = s // GRID_W
    kh = min(NA_WIN_ROWS, rows)
    kw = NA_WIN_COLS
    qg = q.reshape(b, rows, GRID_W, h, d)
    kg = k.reshape(b, rows, GRID_W, h, d)
    vg = v.reshape(b, rows, GRID_W, h, d)
    r = jnp.arange(rows)
    row_start = jnp.clip(r - kh // 2, 0, rows - kh)
    row_idx = row_start[:, None] + jnp.arange(kh)[None, :]
    k_rows = kg[:, row_idx]
    v_rows = vg[:, row_idx]
    c = jnp.arange(GRID_W)
    col_start = jnp.clip(c - kw // 2, 0, GRID_W - kw)
    col_in = (c[None, :] >= col_start[:, None]) & (c[None, :] < col_start[:, None] + kw)
    dr = row_idx - r[:, None] + (NA_WIN_ROWS - 1)
    dc = jnp.clip(c[None, :] - c[:, None] + (kw - 1), 0, 2 * kw - 2)
    bias = rpb[:, dr[:, None, :, None], dc[None, :, None, :]]
    sc = jnp.einsum('brqhd,brikhd->bhrqik', qg, k_rows, preferred_element_type=jnp.float32) * (d ** -0.5)
    sc = sc + bias[None].astype(jnp.float32)
    sc = jnp.where(col_in[:, None, :], sc, -jnp.inf)
    p = jax.nn.softmax(sc.reshape(b, h, rows, GRID_W, kh * GRID_W), axis=-1)
    p = p.reshape(b, h, rows, GRID_W, kh, GRID_W).astype(v.dtype)
    out = jnp.einsum('bhrqik,brikhd->brqhd', p, v_rows)
    return out.reshape(b, s, h, d)


def _hybrid_layer(x, layer_idx, w_in, w_uq, q_norm, w_ukv, kv_norm, na_rpb,
                  lam_q1, lam_k1, lam_q2, lam_k2, diff_subln,
                  w_o_a, w_o_b, w_o_c, b_merge, w_out, ln_g, ln_b):
    b, s, _ = x.shape
    pos = jnp.arange(s)
    h = jnp.einsum('bsd,dn->bsn', x, w_in)
    (a_q, a_k, a_v, a_gate,
     b_cq, b_ckv, b_krope, b_gate,
     c_q, c_k, c_v, c_gate,
     merge_logits) = _split_cols(h)

    ya = _neighbourhood_attention(a_q.reshape(b, s, NA_HEADS, HEAD_DIM),
                                  a_k.reshape(b, s, NA_HEADS, HEAD_DIM),
                                  a_v.reshape(b, s, NA_HEADS, HEAD_DIM), na_rpb)
    ya = ya.reshape(b, s, WIDTH_A) * jax.nn.silu(a_gate)

    cq = _rms_norm(b_cq, q_norm)
    qb = jnp.einsum('bsr,rn->bsn', cq, w_uq).reshape(b, s, MLA_HEADS, MLA_NOPE + MLA_ROPE)
    qb = jnp.concatenate([qb[..., :MLA_NOPE], _rope(qb[..., MLA_NOPE:], pos)], axis=-1)
    ckv = _rms_norm(b_ckv, kv_norm)
    kv = jnp.einsum('bsr,rn->bsn', ckv, w_ukv).reshape(b, s, MLA_HEADS, MLA_NOPE + MLA_V)
    k_nope, vb = kv[..., :MLA_NOPE], kv[..., MLA_NOPE:]
    k_rope = _rope(b_krope.reshape(b, s, 1, MLA_ROPE), pos)
    kb = jnp.concatenate([k_nope, jnp.broadcast_to(k_rope, (b, s, MLA_HEADS, MLA_ROPE))], axis=-1)
    yb = _dense_attention(qb, kb, vb, (MLA_NOPE + MLA_ROPE) ** -0.5)
    yb = yb.reshape(b, s, WIDTH_B) * jax.nn.silu(b_gate)

    qc = _rope(c_q.reshape(b, s, DIFF_HEADS * 2, DIFF_QK), pos).reshape(b, s, DIFF_HEADS, 2, DIFF_QK)
    kc = _rope(c_k.reshape(b, s, DIFF_HEADS * 2, DIFF_QK), pos).reshape(b, s, DIFF_HEADS, 2, DIFF_QK)
    vc = c_v.reshape(b, s, DIFF_HEADS, DIFF_V)
    lam_init = 0.8 - 0.6 * math.exp(-0.3 * layer_idx)
    lam = (jnp.exp(jnp.sum(lam_q1.astype(jnp.float32) * lam_k1.astype(jnp.float32)))
           - jnp.exp(jnp.sum(lam_q2.astype(jnp.float32) * lam_k2.astype(jnp.float32))) + lam_init)
    yc = _diff_attention(qc[:, :, :, 0], kc[:, :, :, 0], qc[:, :, :, 1], kc[:, :, :, 1], vc, lam, DIFF_QK ** -0.5)
    yc = _rms_norm(yc, diff_subln) * (1.0 - lam_init)
    yc = yc.reshape(b, s, WIDTH_C) * jax.nn.silu(c_gate)

    g = jax.nn.sigmoid(merge_logits + b_merge).reshape(b, s, N_BRANCH, D_MODEL)
    merged = (g[:, :, 0] * jnp.einsum('bsw,wd->bsd', ya, w_o_a)
              + g[:, :, 1] * jnp.einsum('bsw,wd->bsd', yb, w_o_b)
              + g[:, :, 2] * jnp.einsum('bsw,wd->bsd', yc, w_o_c))
    y = jnp.einsum('bsd,de->bse', merged, w_out)

    return _layer_norm(DEEPNORM_ALPHA * x + y, ln_g, ln_b)


def setup_inputs(seed: int = 0) -> dict:
    key = jax.random.key(seed)
    ks = jax.random.split(key, 20)
    f32 = jnp.float32

    def nrm(k, shape, scale):
        return jax.random.normal(k, shape, f32) * scale

    return {
        "x": nrm(ks[0], (BATCH, SEQ, D_MODEL), 1.0),
        "w_in": nrm(ks[1], (DEPTH, D_MODEL, IN_WIDTH), D_MODEL ** -0.5),
        "w_uq": nrm(ks[2], (DEPTH, MLA_Q_RANK, MLA_HEADS * (MLA_NOPE + MLA_ROPE)), MLA_Q_RANK ** -0.5),
        "q_norm": 1.0 + nrm(ks[3], (DEPTH, MLA_Q_RANK), 0.02),
        "w_ukv": nrm(ks[4], (DEPTH, MLA_KV_RANK, MLA_HEADS * (MLA_NOPE + MLA_V)), MLA_KV_RANK ** -0.5),
        "kv_norm": 1.0 + nrm(ks[5], (DEPTH, MLA_KV_RANK), 0.02),
        "na_rpb": nrm(ks[6], (DEPTH, NA_HEADS, 2 * NA_WIN_ROWS - 1, 2 * NA_WIN_COLS - 1), 0.05),
        "lam_q1": nrm(ks[7], (DEPTH, DIFF_QK), 0.1),
        "lam_k1": nrm(ks[8], (DEPTH, DIFF_QK), 0.1),
        "lam_q2": nrm(ks[9], (DEPTH, DIFF_QK), 0.1),
        "lam_k2": nrm(ks[10], (DEPTH, DIFF_QK), 0.1),
        "diff_subln": 1.0 + nrm(ks[11], (DEPTH, DIFF_V), 0.02),
        "w_o_a": nrm(ks[12], (DEPTH, WIDTH_A, D_MODEL), WIDTH_A ** -0.5 * DEEPNORM_BETA),
        "w_o_b": nrm(ks[13], (DEPTH, WIDTH_B, D_MODEL), WIDTH_B ** -0.5 * DEEPNORM_BETA),
        "w_o_c": nrm(ks[14], (DEPTH, WIDTH_C, D_MODEL), WIDTH_C ** -0.5 * DEEPNORM_BETA),
        "b_merge": nrm(ks[15], (DEPTH, N_BRANCH * D_MODEL), 0.02),
        "w_out": nrm(ks[16], (DEPTH, D_MODEL, D_MODEL), D_MODEL ** -0.5 * DEEPNORM_BETA),
        "ln_g": 1.0 + nrm(ks[17], (DEPTH, D_MODEL), 0.02),
        "ln_b": nrm(ks[18], (DEPTH, D_MODEL), 0.02),
    }


def reference(x, w_in, w_uq, q_norm, w_ukv, kv_norm, na_rpb, lam_q1, lam_k1, lam_q2, lam_k2,
              diff_subln, w_o_a, w_o_b, w_o_c, b_merge, w_out, ln_g, ln_b):
    for l in range(DEPTH):
        x = _hybrid_layer(x, l, w_in[l], w_uq[l], q_norm[l], w_ukv[l], kv_norm[l], na_rpb[l],
                          lam_q1[l], lam_k1[l], lam_q2[l], lam_k2[l], diff_subln[l],
                          w_o_a[l], w_o_b[l], w_o_c[l], b_merge[l], w_out[l], ln_g[l], ln_b[l])
    return x
```

```python
import functools
import math

import jax
import jax.numpy as jnp
import numpy as np
from jax import lax
from jax.experimental import pallas as pl
from jax.experimental.pallas import tpu as pltpu

GRID_W = 64
HEAD_DIM = 128
ROPE_THETA = 10000.0
LN_EPS = 1e-5
RMS_EPS = 1e-6

NA_HEADS = 8
NA_WIN_ROWS = 8
NA_WIN_COLS = 16
WIDTH_A = NA_HEADS * HEAD_DIM

MLA_HEADS = 8
MLA_Q_RANK = 1536
MLA_KV_RANK = 512
MLA_NOPE = 128
MLA_ROPE = 64
MLA_V = 128
WIDTH_B = MLA_HEADS * MLA_V

DIFF_HEADS = 8
DIFF_QK = 64
DIFF_V = 2 * DIFF_QK
WIDTH_C = DIFF_HEADS * DIFF_V

N_BRANCH = 3

LANE = 128
MLA_QK_PAD = 2 * LANE
NEG_BIG = -1e30
VMEM_CAP = 60 * 1024 * 1024

OFF_BCQ = 0
OFF_BCKV = OFF_BCQ + MLA_Q_RANK
OFF_AQ = OFF_BCKV + MLA_KV_RANK
OFF_AK = OFF_AQ + WIDTH_A
OFF_AV = OFF_AK + WIDTH_A
OFF_AG = OFF_AV + WIDTH_A
OFF_BG = OFF_AG + WIDTH_A
OFF_CQ = OFF_BG + WIDTH_B
OFF_CK = OFF_CQ + WIDTH_C
OFF_CV = OFF_CK + WIDTH_C
OFF_CG = OFF_CV + WIDTH_C
OFF_MERGE = OFF_CG + WIDTH_C

NA_GROUP_ROWS = 4
NA_KEY_ROWS = NA_GROUP_ROWS + NA_WIN_ROWS

MODE_SCALE, MODE_SILU, MODE_SIGMOID, MODE_ROPE = 0, 1, 2, 3


def _vmem_params(block_bytes, temp_bytes=0):
    need = 2 * block_bytes + temp_bytes + (4 << 20)
    return pltpu.CompilerParams(vmem_limit_bytes=int(min(max(need, 16 << 20), VMEM_CAP)))


def _tile(n, pref, quantum):
    t = min(pref, n)
    t -= t % quantum
    while t > quantum and n % t:
        t -= quantum
    assert t >= quantum and n % t == 0, (n, pref, quantum)
    return t


def _rope_slab(xs, cos, sin_signed):
    lane = lax.broadcasted_iota(jnp.int32, xs.shape, 1)
    first_half = (lane % (2 * (MLA_ROPE // 2))) < (MLA_ROPE // 2)
    partner = jnp.where(first_half, pltpu.roll(xs, LANE - MLA_ROPE // 2, 1), pltpu.roll(xs, MLA_ROPE // 2, 1))
    return xs * cos + partner * sin_signed


def _proj_kernel(scale_ref, x_ref, w_ref, b_ref, cos_ref, sin_ref, o_ref, *, mode_ranges):
    j = pl.program_id(1)
    acc = jnp.dot(x_ref[...], w_ref[...], preferred_element_type=jnp.float32)
    scale = scale_ref[j]
    for mode, lo, hi in mode_ranges:
        @pl.when((j >= lo) & (j < hi))
        def _(mode=mode):
            if mode == MODE_SCALE:
                o_ref[...] = (acc * scale).astype(o_ref.dtype)
            elif mode == MODE_SILU:
                o_ref[...] = (acc * jax.nn.sigmoid(acc)).astype(o_ref.dtype)
            elif mode == MODE_SIGMOID:
                o_ref[...] = jax.nn.sigmoid(acc + b_ref[...]).astype(o_ref.dtype)
            else:
                cos = cos_ref[...]
                sin = sin_ref[...]
                for s in range(acc.shape[1] // LANE):
                    sl = slice(s * LANE, (s + 1) * LANE)
                    o_ref[:, sl] = (_rope_slab(acc[:, sl], cos, sin) * scale).astype(o_ref.dtype)


def _project(x_bf, w_bf, bias, scales, mode_ranges, cos_t, sin_t, seq, tm_pref, tn):
    t, k = x_bf.shape
    n = w_bf.shape[1]
    tm = _tile(math.gcd(t, seq), tm_pref, 8)
    nj = n // tn
    pos_blocks = seq // tm
    block_bytes = (tm * k + k * tn + tm * tn) * 2 + tn * 4 + 2 * tm * LANE * 4
    return pl.pallas_call(
        functools.partial(_proj_kernel, mode_ranges=mode_ranges),
        grid=(t // tm, nj),
        in_specs=[
            pl.BlockSpec(memory_space=pltpu.SMEM),
            pl.BlockSpec((tm, k), lambda i, j: (i, 0)),
            pl.BlockSpec((k, tn), lambda i, j: (0, j)),
            pl.BlockSpec((1, tn), lambda i, j: (0, j)),
            pl.BlockSpec((tm, LANE), lambda i, j: (i % pos_blocks, 0)),
            pl.BlockSpec((tm, LANE), lambda i, j: (i % pos_blocks, 0)),
        ],
        out_specs=pl.BlockSpec((tm, tn), lambda i, j: (i, j)),
        out_shape=jax.ShapeDtypeStruct((t, n), jnp.bfloat16),
        compiler_params=_vmem_params(block_bytes, 3 * tm * tn * 4),
        name="project",
    )(scales, x_bf, w_bf, bias, cos_t, sin_t)


def _rms_normed(h_ref, g_ref):
    c = h_ref[...].astype(jnp.float32)
    ms = jnp.mean(c * c, axis=-1, keepdims=True)
    return (c * lax.rsqrt(ms + RMS_EPS) * g_ref[...]).astype(jnp.bfloat16)


def _mla_q_kernel(h_ref, g_ref, w_ref, cos_ref, sin_ref, o_ref, *, scale):
    acc = jnp.dot(_rms_normed(h_ref, g_ref), w_ref[...], preferred_element_type=jnp.float32)
    cos = cos_ref[...]
    sin = sin_ref[...]
    for h in range(MLA_HEADS):
        base = h * MLA_QK_PAD
        o_ref[:, base:base + LANE] = (acc[:, base:base + LANE] * scale).astype(o_ref.dtype)
        rope = _rope_slab(acc[:, base + LANE:base + 2 * LANE], cos, sin)
        o_ref[:, base + LANE:base + 2 * LANE] = (rope * scale).astype(o_ref.dtype)


def _mla_q(h, q_norm, w_uq_pad, cos_t, sin_t, seq):
    t = h.shape[0]
    tm = _tile(math.gcd(t, seq), 512, 8)
    pos_blocks = seq // tm
    n = MLA_HEADS * MLA_QK_PAD
    block_bytes = (tm * MLA_Q_RANK + MLA_Q_RANK * n + tm * n) * 2 + 2 * tm * LANE * 4
    return pl.pallas_call(
        functools.partial(_mla_q_kernel, scale=(MLA_NOPE + MLA_ROPE) ** -0.5),
        grid=(t // tm,),
        in_specs=[
            pl.BlockSpec((tm, MLA_Q_RANK), lambda i: (i, OFF_BCQ // MLA_Q_RANK)),
            pl.BlockSpec((1, MLA_Q_RANK), lambda i: (0, 0)),
            pl.BlockSpec((MLA_Q_RANK, n), lambda i: (0, 0)),
            pl.BlockSpec((tm, LANE), lambda i: (i % pos_blocks, 0)),
            pl.BlockSpec((tm, LANE), lambda i: (i % pos_blocks, 0)),
        ],
        out_specs=pl.BlockSpec((tm, n), lambda i: (i, 0)),
        out_shape=jax.ShapeDtypeStruct((t, n), jnp.bfloat16),
        compiler_params=_vmem_params(block_bytes, tm * (2 * n + 3 * MLA_Q_RANK) * 4),
        name="mla_q",
    )(h, q_norm, w_uq_pad, cos_t, sin_t)


def _mla_kv_kernel(h_ref, g_ref, w_ref, kr_ref, k_ref, v_ref):
    acc = jnp.dot(_rms_normed(h_ref, g_ref), w_ref[...], preferred_element_type=jnp.float32)
    kr = kr_ref[...]
    for h in range(MLA_HEADS):
        base = h * MLA_QK_PAD
        k_ref[:, base:base + LANE] = acc[:, h * MLA_NOPE:(h + 1) * MLA_NOPE].astype(k_ref.dtype)
        k_ref[:, base + LANE:base + 2 * LANE] = kr
    v_ref[...] = acc[:, MLA_HEADS * MLA_NOPE:].astype(v_ref.dtype)


def _mla_kv(h, kv_norm, w_ukv_r, k_rope):
    t = h.shape[0]
    tm = _tile(t, 512, 8)
    nk = MLA_HEADS * MLA_QK_PAD
    nw = MLA_HEADS * (MLA_NOPE + MLA_V)
    block_bytes = (tm * MLA_KV_RANK + MLA_KV_RANK * nw + tm * LANE + tm * nk + tm * WIDTH_B) * 2
    return pl.pallas_call(
        _mla_kv_kernel,
        grid=(t // tm,),
        in_specs=[
            pl.BlockSpec((tm, MLA_KV_RANK), lambda i: (i, OFF_BCKV // MLA_KV_RANK)),
            pl.BlockSpec((1, MLA_KV_RANK), lambda i: (0, 0)),
            pl.BlockSpec((MLA_KV_RANK, nw), lambda i: (0, 0)),
            pl.BlockSpec((tm, LANE), lambda i: (i, 0)),
        ],
        out_specs=[
            pl.BlockSpec((tm, nk), lambda i: (i, 0)),
            pl.BlockSpec((tm, WIDTH_B), lambda i: (i, 0)),
        ],
        out_shape=[
            jax.ShapeDtypeStruct((t, nk), jnp.bfloat16),
            jax.ShapeDtypeStruct((t, WIDTH_B), jnp.bfloat16),
        ],
        compiler_params=_vmem_params(block_bytes, tm * (nw + 2 * MLA_KV_RANK) * 4),
        name="mla_kv",
    )(h, kv_norm, w_ukv_r, k_rope)


def _qk_scores(q, k):
    return lax.dot_general(q, k, (((1,), (1,)), ((), ())), preferred_element_type=jnp.float32)


def _attn_b_kernel(q_ref, k_ref, v_ref, g_ref, o_ref):
    s = _qk_scores(q_ref[...], k_ref[...])
    m = jnp.max(s, axis=-1, keepdims=True)
    p = jnp.exp(s - m)
    l = jnp.sum(p, axis=-1, keepdims=True)
    o = jnp.dot(p.astype(jnp.bfloat16), v_ref[...], preferred_element_type=jnp.float32)
    o_ref[...] = (o / l * g_ref[...].astype(jnp.float32)).astype(o_ref.dtype)


def _attn_b(q_pad, k_pad, v, h, batch, seq):
    t = q_pad.shape[0]
    tq = _tile(seq, 512, 8)
    nq = seq // tq
    block_bytes = (tq * MLA_QK_PAD + seq * MLA_QK_PAD + seq * MLA_V + 2 * tq * MLA_V) * 2
    return pl.pallas_call(
        _attn_b_kernel,
        grid=(batch, MLA_HEADS, nq),
        in_specs=[
            pl.BlockSpec((tq, MLA_QK_PAD), lambda b, hd, qi: (b * nq + qi, hd)),
            pl.BlockSpec((seq, MLA_QK_PAD), lambda b, hd, qi: (b, hd)),
            pl.BlockSpec((seq, MLA_V), lambda b, hd, qi: (b, hd)),
            pl.BlockSpec((tq, MLA_V), lambda b, hd, qi: (b * nq + qi, OFF_BG // MLA_V + hd)),
        ],
        out_specs=pl.BlockSpec((tq, MLA_V), lambda b, hd, qi: (b * nq + qi, hd)),
        out_shape=jax.ShapeDtypeStruct((t, WIDTH_B), jnp.bfloat16),
        compiler_params=_vmem_params(block_bytes, 3 * tq * seq * 4),
        name="attn_b",
    )(q_pad, k_pad, v, h)


def _attn_c_kernel(lq1_ref, lk1_ref, lq2_ref, lk2_ref, q_ref, k_ref, v_ref, g_ref, sub_ref, o_ref, *, lam_init):
    lam = (jnp.exp(jnp.sum(lq1_ref[...] * lk1_ref[...], keepdims=True))
           - jnp.exp(jnp.sum(lq2_ref[...] * lk2_ref[...], keepdims=True)) + lam_init)
    q = q_ref[...]
    k = k_ref[...]
    lane = lax.broadcasted_iota(jnp.int32, q.shape, 1)
    zero = jnp.zeros_like(q)
    s1 = _qk_scores(jnp.where(lane < DIFF_QK, q, zero), k)
    s2 = _qk_scores(jnp.where(lane >= DIFF_QK, q, zero), k)
    p1 = jnp.exp(s1 - jnp.max(s1, axis=-1, keepdims=True))
    p2 = jnp.exp(s2 - jnp.max(s2, axis=-1, keepdims=True))
    r1 = 1.0 / jnp.sum(p1, axis=-1, keepdims=True)
    r2 = lam / jnp.sum(p2, axis=-1, keepdims=True)
    p = (p1 * r1 - p2 * r2).astype(jnp.bfloat16)
    o = jnp.dot(p, v_ref[...], preferred_element_type=jnp.float32)
    ms = jnp.mean(o * o, axis=-1, keepdims=True)
    o = o * lax.rsqrt(ms + RMS_EPS) * sub_ref[...] * (1.0 - lam_init)
    o_ref[...] = (o * g_ref[...].astype(jnp.float32)).astype(o_ref.dtype)


def _attn_c(h, lam_q1, lam_k1, lam_q2, lam_k2, diff_subln, layer_idx, batch, seq):
    t = h.shape[0]
    tq = _tile(seq, 256, 8)
    nq = seq // tq
    lam_init = 0.8 - 0.6 * math.exp(-0.3 * layer_idx)
    lam_spec = pl.BlockSpec((1, DIFF_QK), lambda b, hd, qi: (0, 0))
    block_bytes = (3 * tq * DIFF_V + 2 * seq * DIFF_V) * 2
    return pl.pallas_call(
        functools.partial(_attn_c_kernel, lam_init=lam_init),
        grid=(batch, DIFF_HEADS, nq),
        in_specs=[
            lam_spec, lam_spec, lam_spec, lam_spec,
            pl.BlockSpec((tq, DIFF_V), lambda b, hd, qi: (b * nq + qi, OFF_CQ // DIFF_V + hd)),
            pl.BlockSpec((seq, DIFF_V), lambda b, hd, qi: (b, OFF_CK // DIFF_V + hd)),
            pl.BlockSpec((seq, DIFF_V), lambda b, hd, qi: (b, OFF_CV // DIFF_V + hd)),
            pl.BlockSpec((tq, DIFF_V), lambda b, hd, qi: (b * nq + qi, OFF_CG // DIFF_V + hd)),
            pl.BlockSpec((1, DIFF_V), lambda b, hd, qi: (0, 0)),
        ],
        out_specs=pl.BlockSpec((tq, DIFF_V), lambda b, hd, qi: (b * nq + qi, hd)),
        out_shape=jax.ShapeDtypeStruct((t, WIDTH_C), jnp.bfloat16),
        compiler_params=_vmem_params(block_bytes, 6 * tq * seq * 4),
        name="attn_c",
    )(lam_q1, lam_k1, lam_q2, lam_k2, h, h, h, h, diff_subln)


def _na_bias_table(rpb, rows):
    n_groups = rows // NA_GROUP_ROWS
    kh = NA_WIN_ROWS
    a = np.arange(NA_GROUP_ROWS)[:, None]
    i = np.arange(NA_KEY_ROWS)[None, :]
    dr_list, row_ok_list = [], []
    for grp in (0, 1, n_groups - 1):
        k_start = NA_GROUP_ROWS * min(max(grp - 1, 0), n_groups - 3)
        r = NA_GROUP_ROWS * grp + a
        row_start = np.clip(r - kh // 2, 0, rows - kh)
        k_row = k_start + i
        row_ok_list.append((k_row >= row_start) & (k_row < row_start + kh))
        dr_list.append(np.clip(k_row - r + (NA_WIN_ROWS - 1), 0, 2 * NA_WIN_ROWS - 2))
    dr = np.stack(dr_list)
    row_ok = np.stack(row_ok_list)
    c = np.arange(GRID_W)
    col_start = np.clip(c - NA_WIN_COLS // 2, 0, GRID_W - NA_WIN_COLS)
    col_ok = (c[None, :] >= col_start[:, None]) & (c[None, :] < col_start[:, None] + NA_WIN_COLS)
    dc = np.clip(c[None, :] - c[:, None] + (NA_WIN_COLS - 1), 0, 2 * NA_WIN_COLS - 2)
    shape = (3, NA_GROUP_ROWS, GRID_W, NA_KEY_ROWS, GRID_W)
    dr_full = np.broadcast_to(dr[:, :, None, :, None], shape)
    dc_full = np.broadcast_to(dc[None, None, :, None, :], shape)
    ok = np.broadcast_to(row_ok[:, :, None, :, None] & col_ok[None, None, :, None, :], shape)
    bias = rpb.astype(jnp.float32)[:, dr_full, dc_full]
    bias = jnp.where(ok[None], bias, NEG_BIG)
    bias = jnp.moveaxis(bias, 0, 1)
    return bias.reshape(3, NA_HEADS, NA_GROUP_ROWS * GRID_W, NA_KEY_ROWS * GRID_W)


def _attn_a_kernel(q_ref, k_ref, v_ref, g_ref, bias_ref, o_ref, *, n_groups, scale):
    j = pl.program_id(2)
    gq = q_ref.shape[0]
    gk = bias_ref.shape[-1]
    start = pl.multiple_of(jnp.clip(j - 1, 0, n_groups - 3) * gq, gq)
    s = _qk_scores(q_ref[...], k_ref[pl.ds(start, gk), :]) * scale + bias_ref[0, 0]
    m = jnp.max(s, axis=-1, keepdims=True)
    p = jnp.exp(s - m)
    l = jnp.sum(p, axis=-1, keepdims=True)
    o = jnp.dot(p.astype(jnp.bfloat16), v_ref[pl.ds(start, gk), :], preferred_element_type=jnp.float32)
    o_ref[...] = (o / l * g_ref[...].astype(jnp.float32)).astype(o_ref.dtype)


def _attn_a(h, bias, batch, seq):
    t = h.shape[0]
    gq = NA_GROUP_ROWS * GRID_W
    gk = NA_KEY_ROWS * GRID_W
    n_groups = seq // gq
    assert n_groups >= 3

    def variant(j):
        return jnp.where(j == 0, 0, jnp.where(j == n_groups - 1, 2, 1))

    block_bytes = (3 * gq * HEAD_DIM + 2 * seq * HEAD_DIM) * 2 + gq * gk * 4
    return pl.pallas_call(
        functools.partial(_attn_a_kernel, n_groups=n_groups, scale=HEAD_DIM ** -0.5),
        grid=(batch, NA_HEADS, n_groups),
        in_specs=[
            pl.BlockSpec((gq, HEAD_DIM), lambda b, hd, j: (b * n_groups + j, OFF_AQ // HEAD_DIM + hd)),
            pl.BlockSpec((seq, HEAD_DIM), lambda b, hd, j: (b, OFF_AK // HEAD_DIM + hd)),
            pl.BlockSpec((seq, HEAD_DIM), lambda b, hd, j: (b, OFF_AV // HEAD_DIM + hd)),
            pl.BlockSpec((gq, HEAD_DIM), lambda b, hd, j: (b * n_groups + j, OFF_AG // HEAD_DIM + hd)),
            pl.BlockSpec((1, 1, gq, gk), lambda b, hd, j: (variant(j), hd, 0, 0)),
        ],
        out_specs=pl.BlockSpec((gq, HEAD_DIM), lambda b, hd, j: (b * n_groups + j, hd)),
        out_shape=jax.ShapeDtypeStruct((t, WIDTH_A), jnp.bfloat16),
        compiler_params=_vmem_params(block_bytes, 4 * gq * gk * 4),
        name="attn_a",
    )(h, h, h, h, bias)


def _merge_kernel(ya_ref, yb_ref, yc_ref, wa_ref, wb_ref, wc_ref, ga_ref, gb_ref, gc_ref, o_ref):
    def branch(y_ref, w_ref, g_ref):
        return g_ref[...].astype(jnp.float32) * jnp.dot(y_ref[...], w_ref[...], preferred_element_type=jnp.float32)
    merged = branch(ya_ref, wa_ref, ga_ref) + branch(yb_ref, wb_ref, gb_ref) + branch(yc_ref, wc_ref, gc_ref)
    o_ref[...] = merged.astype(o_ref.dtype)


def _merge(ya, yb, yc, w_o_a, w_o_b, w_o_c, h, d_model):
    t = ya.shape[0]
    tm = _tile(t, 512, 8)
    tn = _tile(d_model, 1024, LANE)
    y_spec = pl.BlockSpec((tm, WIDTH_A), lambda i, j: (i, 0))
    w_spec = pl.BlockSpec((WIDTH_A, tn), lambda i, j: (0, j))

    def gate_spec(branch):
        off = (OFF_MERGE + branch * d_model) // tn
        return pl.BlockSpec((tm, tn), lambda i, j: (i, off + j))

    assert OFF_MERGE % tn == 0 and d_model % tn == 0
    block_bytes = (3 * tm * WIDTH_A + 3 * WIDTH_A * tn + 4 * tm * tn) * 2
    return pl.pallas_call(
        _merge_kernel,
        grid=(t // tm, d_model // tn),
        in_specs=[y_spec, y_spec, y_spec, w_spec, w_spec, w_spec, gate_spec(0), gate_spec(1), gate_spec(2)],
        out_specs=pl.BlockSpec((tm, tn), lambda i, j: (i, j)),
        out_shape=jax.ShapeDtypeStruct((t, d_model), jnp.bfloat16),
        compiler_params=_vmem_params(block_bytes, 4 * tm * tn * 4),
        name="merge",
    )(ya, yb, yc, w_o_a, w_o_b, w_o_c, h, h, h)


def _out_kernel(m_ref, w_ref, x_ref, o_ref, *, alpha):
    y = jnp.dot(m_ref[...], w_ref[...], preferred_element_type=jnp.float32)
    o_ref[...] = alpha * x_ref[...] + y


def _out_proj(merged, w_out, x, alpha):
    t, d = x.shape
    tm = _tile(t, 512, 8)
    tn = _tile(d, 1024, LANE)
    block_bytes = (tm * d + d * tn) * 2 + 2 * tm * tn * 4
    return pl.pallas_call(
        functools.partial(_out_kernel, alpha=alpha),
        grid=(t // tm, d // tn),
        in_specs=[
            pl.BlockSpec((tm, d), lambda i, j: (i, 0)),
            pl.BlockSpec((d, tn), lambda i, j: (0, j)),
            pl.BlockSpec((tm, tn), lambda i, j: (i, j)),
        ],
        out_specs=pl.BlockSpec((tm, tn), lambda i, j: (i, j)),
        out_shape=jax.ShapeDtypeStruct((t, d), jnp.float32),
        compiler_params=_vmem_params(block_bytes, 2 * tm * tn * 4),
        name="out_proj",
    )(merged, w_out, x)


def _ln_kernel(z_ref, g_ref, b_ref, o_ref, obf_ref):
    z = z_ref[...]
    mu = jnp.mean(z, axis=-1, keepdims=True)
    zc = z - mu
    var = jnp.mean(zc * zc, axis=-1, keepdims=True)
    out = zc * lax.rsqrt(var + LN_EPS) * g_ref[...] + b_ref[...]
    o_ref[...] = out
    obf_ref[...] = out.astype(obf_ref.dtype)


def _layer_norm(z, ln_g, ln_b):
    t, d = z.shape
    tm = _tile(t, 256, 8)
    row = pl.BlockSpec((tm, d), lambda i: (i, 0))
    vec = pl.BlockSpec((1, d), lambda i: (0, 0))
    return pl.pallas_call(
        _ln_kernel,
        grid=(t // tm,),
        in_specs=[row, vec, vec],
        out_specs=[row, row],
        out_shape=[jax.ShapeDtypeStruct((t, d), jnp.float32), jax.ShapeDtypeStruct((t, d), jnp.bfloat16)],
        compiler_params=_vmem_params(tm * d * 10, 3 * tm * d * 4),
        name="layer_norm",
    )(z, ln_g, ln_b)


def _rope_tables(seq):
    half = MLA_ROPE // 2
    inv_freq = ROPE_THETA ** (-jnp.arange(half, dtype=jnp.float32) * 2.0 / MLA_ROPE)
    ang = jnp.arange(seq, dtype=jnp.float32)[:, None] * inv_freq[None, :]
    cos = jnp.cos(ang)
    sin = jnp.sin(ang)
    cos_t = jnp.concatenate([cos, cos, cos, cos], axis=-1)
    sin_t = jnp.concatenate([-sin, sin, -sin, sin], axis=-1)
    return cos_t, sin_t


def _in_weight_layout(w_in_l, d_model):
    splits = (WIDTH_A, WIDTH_A, WIDTH_A, WIDTH_A, MLA_Q_RANK, MLA_KV_RANK, MLA_ROPE, WIDTH_B,
              WIDTH_C, WIDTH_C, WIDTH_C, WIDTH_C, N_BRANCH * d_model)
    points = np.cumsum(np.array(splits))[:-1].tolist()
    (a_q, a_k, a_v, a_g, b_cq, b_ckv, b_kr, b_g, c_q, c_k, c_v, c_g, mg) = jnp.split(w_in_l, points, axis=1)
    main = jnp.concatenate([b_cq, b_ckv, a_q, a_k, a_v, a_g, b_g, c_q, c_k, c_v, c_g, mg], axis=1)
    kr = jnp.pad(b_kr, ((0, 0), (0, LANE - MLA_ROPE)))
    return main.astype(jnp.bfloat16), kr.astype(jnp.bfloat16)


def _layer(x, x_bf, layer_idx, depth, batch, seq, cos_t, sin_t, w_in, w_uq, q_norm, w_ukv, kv_norm, na_rpb,
           lam_q1, lam_k1, lam_q2, lam_k2, diff_subln, w_o_a, w_o_b, w_o_c, b_merge, w_out, ln_g, ln_b):
    t, d_model = x.shape
    f32 = jnp.float32
    bf16 = jnp.bfloat16
    tn = 1024
    assert OFF_MERGE % tn == 0 and d_model % tn == 0

    w_main, w_kr = _in_weight_layout(w_in, d_model)
    n_main = w_main.shape[1]
    nj = n_main // tn
    blk = lambda off: off // tn
    mode_ranges = (
        (MODE_SCALE, 0, blk(OFF_AG)),
        (MODE_SILU, blk(OFF_AG), blk(OFF_CQ)),
        (MODE_ROPE, blk(OFF_CQ), blk(OFF_CV)),
        (MODE_SCALE, blk(OFF_CV), blk(OFF_CG)),
        (MODE_SILU, blk(OFF_CG), blk(OFF_MERGE)),
        (MODE_SIGMOID, blk(OFF_MERGE), nj),
    )
    scales = np.ones((nj,), np.float32)
    scales[blk(OFF_CQ)] = DIFF_QK ** -0.5
    bias = jnp.concatenate([jnp.zeros((OFF_MERGE,), f32), b_merge.astype(f32)])[None, :]
    h = _project(x_bf, w_main, bias, jnp.asarray(scales), mode_ranges, cos_t, sin_t, seq, 1024, tn)

    k_rope = _project(x_bf, w_kr, jnp.zeros((1, LANE), f32), jnp.ones((1,), f32),
                      ((MODE_ROPE, 0, 1),), cos_t, sin_t, seq, 1024, LANE)

    w_uq_pad = jnp.pad(w_uq.reshape(MLA_Q_RANK, MLA_HEADS, MLA_NOPE + MLA_ROPE),
                       ((0, 0), (0, 0), (0, MLA_QK_PAD - MLA_NOPE - MLA_ROPE)))
    w_uq_pad = w_uq_pad.reshape(MLA_Q_RANK, MLA_HEADS * MLA_QK_PAD).astype(bf16)
    w_ukv_r = w_ukv.reshape(MLA_KV_RANK, MLA_HEADS, MLA_NOPE + MLA_V)
    w_ukv_r = jnp.concatenate([w_ukv_r[:, :, :MLA_NOPE].reshape(MLA_KV_RANK, -1),
                               w_ukv_r[:, :, MLA_NOPE:].reshape(MLA_KV_RANK, -1)], axis=1).astype(bf16)
    q_pad = _mla_q(h, q_norm.astype(f32)[None, :], w_uq_pad, cos_t, sin_t, seq)
    k_pad, v_b = _mla_kv(h, kv_norm.astype(f32)[None, :], w_ukv_r, k_rope)
    yb = _attn_b(q_pad, k_pad, v_b, h, batch, seq)

    yc = _attn_c(h, lam_q1.astype(f32)[None, :], lam_k1.astype(f32)[None, :], lam_q2.astype(f32)[None, :],
                 lam_k2.astype(f32)[None, :], diff_subln.astype(f32)[None, :], layer_idx, batch, seq)

    ya = _attn_a(h, _na_bias_table(na_rpb, seq // GRID_W), batch, seq)

    merged = _merge(ya, yb, yc, w_o_a.astype(bf16), w_o_b.astype(bf16), w_o_c.astype(bf16), h, d_model)
    z = _out_proj(merged, w_out.astype(bf16), x, (2.0 * depth) ** 0.25)
    return _layer_norm(z, ln_g.astype(f32)[None, :], ln_b.astype(f32)[None, :])


def kernel(x, w_in, w_uq, q_norm, w_ukv, kv_norm, na_rpb, lam_q1, lam_k1, lam_q2, lam_k2, diff_subln,
           w_o_a, w_o_b, w_o_c, b_merge, w_out, ln_g, ln_b):
    batch, seq, d_model = x.shape
    depth = w_in.shape[0]
    assert seq % (NA_GROUP_ROWS * GRID_W) == 0 and seq // GRID_W >= NA_WIN_ROWS + NA_GROUP_ROWS
    cos_t, sin_t = _rope_tables(seq)
    xf = x.reshape(batch * seq, d_model)
    x_bf = xf.astype(jnp.bfloat16)
    for l in range(depth):
        xf, x_bf = _layer(xf, x_bf, l, depth, batch, seq, cos_t, sin_t, w_in[l], w_uq[l], q_norm[l], w_ukv[l],
                          kv_norm[l], na_rpb[l], lam_q1[l], lam_k1[l], lam_q2[l], lam_k2[l], diff_subln[l],
                          w_o_a[l], w_o_b[l], w_o_c[l], b_merge[l], w_out[l], ln_g[l], ln_b[l])
    return xf.reshape(batch, seq, d_model)
```

```python
import functools
import math

import jax
import jax.numpy as jnp
import numpy as np
from jax import lax
from jax.experimental import pallas as pl
from jax.experimental.pallas import tpu as pltpu

GRID_W = 64
HEAD_DIM = 128
ROPE_THETA = 10000.0
LN_EPS = 1e-5
RMS_EPS = 1e-6

NA_HEADS = 8
NA_WIN_ROWS = 8
NA_WIN_COLS = 16
WIDTH_A = NA_HEADS * HEAD_DIM

MLA_HEADS = 8
MLA_Q_RANK = 1536
MLA_KV_RANK = 512
MLA_NOPE = 128
MLA_ROPE = 64
MLA_V = 128
WIDTH_B = MLA_HEADS * MLA_V

DIFF_HEADS = 8
DIFF_QK = 64
DIFF_V = 2 * DIFF_QK
WIDTH_C = DIFF_HEADS * DIFF_V

N_BRANCH = 3

LANE = 128
MXU_WIDTH = 256
MLA_QK_PAD = 2 * LANE
NEG_BIG = -1e30
LOG2E = math.log2(math.e)
VMEM_CAP = 60 * 1024 * 1024

OFF_BCQ = 0
OFF_BCKV = OFF_BCQ + MLA_Q_RANK
OFF_AQ = OFF_BCKV + MLA_KV_RANK
OFF_AK = OFF_AQ + WIDTH_A
OFF_AV = OFF_AK + WIDTH_A
OFF_AG = OFF_AV + WIDTH_A
OFF_BG = OFF_AG + WIDTH_A
OFF_CQ = OFF_BG + WIDTH_B
OFF_CK = OFF_CQ + WIDTH_C
OFF_CV = OFF_CK + WIDTH_C
OFF_CG = OFF_CV + WIDTH_C
OFF_MERGE = OFF_CG + WIDTH_C

ATTN_CHAIN = 2 * MXU_WIDTH
NA_GROUP_ROWS = 4
NA_KEY_ROWS = NA_GROUP_ROWS + NA_WIN_ROWS

MODE_SCALE, MODE_SILU, MODE_SIGMOID, MODE_ROPE = 0, 1, 2, 3


def _vmem_params(block_bytes, temp_bytes=0):
    need = 2 * block_bytes + temp_bytes + (4 << 20)
    return pltpu.CompilerParams(vmem_limit_bytes=int(min(max(need, 16 << 20), VMEM_CAP)))


def _tile(n, pref, quantum):
    t = min(pref, n)
    t -= t % quantum
    while t > quantum and n % t:
        t -= quantum
    assert t >= quantum and n % t == 0, (n, pref, quantum)
    return t


def _rope_slab(xs, cos, sin_signed):
    half = MLA_ROPE // 2
    lane = lax.broadcasted_iota(jnp.int32, xs.shape, 1)
    first_half = (lane % MLA_ROPE) < half
    partner = jnp.where(first_half, pltpu.roll(xs, LANE - half, 1), pltpu.roll(xs, half, 1))
    return xs * cos + partner * sin_signed


def _proj_kernel(scale_ref, x_ref, w_ref, b_ref, cos_ref, sin_ref, o_ref, *ot_ref, mode_ranges, t_block):
    j = pl.program_id(1)
    acc = jnp.dot(x_ref[...], w_ref[...], preferred_element_type=jnp.float32)
    scale = scale_ref[j]
    for mode, lo, hi in mode_ranges:
        @pl.when((j >= lo) & (j < hi))
        def _(mode=mode):
            if mode == MODE_SCALE:
                o_ref[...] = (acc * scale).astype(o_ref.dtype)
            elif mode == MODE_SILU:
                o_ref[...] = (acc * jax.nn.sigmoid(acc)).astype(o_ref.dtype)
            elif mode == MODE_SIGMOID:
                o_ref[...] = jax.nn.sigmoid(acc + b_ref[...]).astype(o_ref.dtype)
            else:
                cos = cos_ref[...]
                sin = sin_ref[...]
                for s in range(acc.shape[1] // LANE):
                    sl = slice(s * LANE, (s + 1) * LANE)
                    o_ref[:, sl] = (_rope_slab(acc[:, sl], cos, sin) * scale).astype(o_ref.dtype)
    if t_block is not None:
        @pl.when(j == t_block)
        def _():
            ot_ref[0][...] = acc.T.astype(ot_ref[0].dtype)


def _project(x_bf, w_bf, bias, scales, mode_ranges, cos_t, sin_t, seq, tm_pref, tn, t_block=None):
    t, k = x_bf.shape
    n = w_bf.shape[1]
    tm = _tile(math.gcd(t, seq), tm_pref, 8)
    nj = n // tn
    pos_blocks = seq // tm
    out_specs = [pl.BlockSpec((tm, tn), lambda i, j: (i, j))]
    out_shape = [jax.ShapeDtypeStruct((t, n), jnp.bfloat16)]
    if t_block is not None:
        out_specs.append(pl.BlockSpec((tn, tm), lambda i, j: (0, i)))
        out_shape.append(jax.ShapeDtypeStruct((tn, t), jnp.bfloat16))
    block_bytes = (tm * k // 2 + k * tn + len(out_specs) * tm * tn) * 2 + tn * 4 + 2 * tm * LANE * 4
    return pl.pallas_call(
        functools.partial(_proj_kernel, mode_ranges=mode_ranges, t_block=t_block),
        grid=(t // tm, nj),
        in_specs=[
            pl.BlockSpec(memory_space=pltpu.SMEM),
            pl.BlockSpec((tm, k), lambda i, j: (i, 0), pipeline_mode=pl.Buffered(1)),
            pl.BlockSpec((k, tn), lambda i, j: (0, j)),
            pl.BlockSpec((1, tn), lambda i, j: (0, j)),
            pl.BlockSpec((tm, LANE), lambda i, j: (i % pos_blocks, 0)),
            pl.BlockSpec((tm, LANE), lambda i, j: (i % pos_blocks, 0)),
        ],
        out_specs=out_specs,
        out_shape=out_shape,
        compiler_params=_vmem_params(block_bytes, 3 * tm * tn * 4),
        name="project",
    )(scales, x_bf, w_bf, bias, cos_t, sin_t)


def _rms_normed(h_ref, g_ref):
    c = h_ref[...].astype(jnp.float32)
    ms = jnp.mean(c * c, axis=-1, keepdims=True)
    return (c * lax.rsqrt(ms + RMS_EPS) * g_ref[...]).astype(jnp.bfloat16)


def _mla_q_kernel(h_ref, g_ref, w_ref, cos_ref, sin_ref, o_ref, *, scale):
    acc = jnp.dot(_rms_normed(h_ref, g_ref), w_ref[...], preferred_element_type=jnp.float32)
    cos = cos_ref[...]
    sin = sin_ref[...]
    for h in range(MLA_HEADS):
        base = h * MLA_QK_PAD
        o_ref[:, base:base + LANE] = (acc[:, base:base + LANE] * scale).astype(o_ref.dtype)
        rope = _rope_slab(acc[:, base + LANE:base + 2 * LANE], cos, sin)
        o_ref[:, base + LANE:base + 2 * LANE] = (rope * scale).astype(o_ref.dtype)


def _mla_q(h, q_norm, w_uq_pad, cos_t, sin_t, seq):
    t = h.shape[0]
    tm = _tile(math.gcd(t, seq), 512, 8)
    pos_blocks = seq // tm
    n = MLA_HEADS * MLA_QK_PAD
    block_bytes = (tm * MLA_Q_RANK + MLA_Q_RANK * n + tm * n) * 2 + 2 * tm * LANE * 4
    return pl.pallas_call(
        functools.partial(_mla_q_kernel, scale=(MLA_NOPE + MLA_ROPE) ** -0.5 * LOG2E),
        grid=(t // tm,),
        in_specs=[
            pl.BlockSpec((tm, MLA_Q_RANK), lambda i: (i, OFF_BCQ // MLA_Q_RANK)),
            pl.BlockSpec((1, MLA_Q_RANK), lambda i: (0, 0)),
            pl.BlockSpec((MLA_Q_RANK, n), lambda i: (0, 0)),
            pl.BlockSpec((tm, LANE), lambda i: (i % pos_blocks, 0)),
            pl.BlockSpec((tm, LANE), lambda i: (i % pos_blocks, 0)),
        ],
        out_specs=pl.BlockSpec((tm, n), lambda i: (i, 0)),
        out_shape=jax.ShapeDtypeStruct((t, n), jnp.bfloat16),
        compiler_params=_vmem_params(block_bytes, tm * (2 * n + 3 * MLA_Q_RANK) * 4),
        name="mla_q",
    )(h, q_norm, w_uq_pad, cos_t, sin_t)


def _mla_kv_kernel(h_ref, g_ref, w_ref, kr_ref, k_ref, vt_ref):
    acc = jnp.dot(_rms_normed(h_ref, g_ref), w_ref[...], preferred_element_type=jnp.float32)
    kr = kr_ref[...]
    for h in range(MLA_HEADS):
        base = h * MLA_QK_PAD
        k_ref[:, base:base + LANE] = acc[:, h * MLA_NOPE:(h + 1) * MLA_NOPE].astype(k_ref.dtype)
        k_ref[:, base + LANE:base + 2 * LANE] = kr
    vt_ref[...] = acc[:, MLA_HEADS * MLA_NOPE:].T.astype(vt_ref.dtype)


def _mla_kv(h, kv_norm, w_ukv_r, k_rope):
    t = h.shape[0]
    tm = _tile(t, 512, LANE)
    nk = MLA_HEADS * MLA_QK_PAD
    nw = MLA_HEADS * (MLA_NOPE + MLA_V)
    block_bytes = (tm * MLA_KV_RANK + MLA_KV_RANK * nw + tm * LANE + tm * nk + tm * WIDTH_B) * 2
    return pl.pallas_call(
        _mla_kv_kernel,
        grid=(t // tm,),
        in_specs=[
            pl.BlockSpec((tm, MLA_KV_RANK), lambda i: (i, OFF_BCKV // MLA_KV_RANK)),
            pl.BlockSpec((1, MLA_KV_RANK), lambda i: (0, 0)),
            pl.BlockSpec((MLA_KV_RANK, nw), lambda i: (0, 0)),
            pl.BlockSpec((tm, LANE), lambda i: (i, 0)),
        ],
        out_specs=[
            pl.BlockSpec((tm, nk), lambda i: (i, 0)),
            pl.BlockSpec((WIDTH_B, tm), lambda i: (0, i)),
        ],
        out_shape=[
            jax.ShapeDtypeStruct((t, nk), jnp.bfloat16),
            jax.ShapeDtypeStruct((WIDTH_B, t), jnp.bfloat16),
        ],
        compiler_params=_vmem_params(block_bytes, tm * (2 * nw + 2 * MLA_KV_RANK) * 4),
        name="mla_kv",
    )(h, kv_norm, w_ukv_r, k_rope)


def _qk_scores(a, b):
    return lax.dot_general(a, b, (((1,), (1,)), ((), ())), preferred_element_type=jnp.float32)


def _softmax_exp2(s):
    p = jnp.exp2(s - jnp.max(s, axis=-1, keepdims=True))
    return p, 1.0 / jnp.sum(p, axis=-1, keepdims=True)


def _keys_major_attention(k, q, vt):
    st = _qk_scores(k, q)
    e = jnp.exp2(st - jnp.max(st, axis=0, keepdims=True))
    rl = 1.0 / jnp.sum(e, axis=0, keepdims=True)
    return jnp.dot(vt, e.astype(jnp.bfloat16), preferred_element_type=jnp.float32) * rl


def _attn_b_kernel(q_ref, k_ref, vt_ref, g_ref, o_ref):
    for c in range(q_ref.shape[0] // ATTN_CHAIN):
        rows = pl.ds(c * ATTN_CHAIN, ATTN_CHAIN)
        ot = _keys_major_attention(k_ref[...], q_ref[rows, :], vt_ref[...])
        o_ref[rows, :] = (ot.T * g_ref[rows, :].astype(jnp.float32)).astype(o_ref.dtype)


def _attn_b(q_pad, k_pad, vt, h, batch, seq):
    t = q_pad.shape[0]
    tq = _tile(seq, 1024, ATTN_CHAIN)
    nq = seq // tq
    block_bytes = (tq * MLA_QK_PAD + seq * MLA_QK_PAD + seq * MLA_V + 2 * tq * MLA_V) * 2
    return pl.pallas_call(
        _attn_b_kernel,
        grid=(batch, MLA_HEADS, nq),
        in_specs=[
            pl.BlockSpec((tq, MLA_QK_PAD), lambda b, hd, qi: (b * nq + qi, hd)),
            pl.BlockSpec((seq, MLA_QK_PAD), lambda b, hd, qi: (b, hd)),
            pl.BlockSpec((MLA_V, seq), lambda b, hd, qi: (hd, b)),
            pl.BlockSpec((tq, MLA_V), lambda b, hd, qi: (b * nq + qi, OFF_BG // MLA_V + hd)),
        ],
        out_specs=pl.BlockSpec((tq, MLA_V), lambda b, hd, qi: (b * nq + qi, hd)),
        out_shape=jax.ShapeDtypeStruct((t, WIDTH_B), jnp.bfloat16),
        compiler_params=_vmem_params(block_bytes, 3 * tq * seq * 4),
        name="attn_b",
    )(q_pad, k_pad, vt, h)


def _attn_c_kernel(lq1_ref, lk1_ref, lq2_ref, lk2_ref, q_ref, k_ref, vt_ref, g_ref, sub_ref, o_ref, *, lam_init):
    lam = (jnp.exp(jnp.sum(lq1_ref[...] * lk1_ref[...], keepdims=True))
           - jnp.exp(jnp.sum(lq2_ref[...] * lk2_ref[...], keepdims=True)) + lam_init)
    for c in range(q_ref.shape[0] // ATTN_CHAIN):
        rows = pl.ds(c * ATTN_CHAIN, ATTN_CHAIN)
        q = q_ref[rows, :]
        lane = lax.broadcasted_iota(jnp.int32, q.shape, 1)
        zero = jnp.zeros_like(q)
        ot1 = _keys_major_attention(k_ref[...], jnp.where(lane < DIFF_QK, q, zero), vt_ref[...])
        ot2 = _keys_major_attention(k_ref[...], jnp.where(lane >= DIFF_QK, q, zero), vt_ref[...])
        ot = ot1 - lam * ot2
        ms = jnp.mean(ot * ot, axis=0, keepdims=True)
        o = (ot * lax.rsqrt(ms + RMS_EPS)).T * sub_ref[...] * (1.0 - lam_init)
        o_ref[rows, :] = (o * g_ref[rows, :].astype(jnp.float32)).astype(o_ref.dtype)


def _attn_c(h, vt, lam_q1, lam_k1, lam_q2, lam_k2, diff_subln, layer_idx, batch, seq):
    t = h.shape[0]
    tq = _tile(seq, 1024, ATTN_CHAIN)
    nq = seq // tq
    lam_init = 0.8 - 0.6 * math.exp(-0.3 * layer_idx)
    lam_spec = pl.BlockSpec((1, DIFF_QK), lambda b, hd, qi: (0, 0))
    block_bytes = (3 * tq * DIFF_V + 2 * seq * DIFF_V) * 2
    return pl.pallas_call(
        functools.partial(_attn_c_kernel, lam_init=lam_init),
        grid=(batch, DIFF_HEADS, nq),
        in_specs=[
            lam_spec, lam_spec, lam_spec, lam_spec,
            pl.BlockSpec((tq, DIFF_V), lambda b, hd, qi: (b * nq + qi, OFF_CQ // DIFF_V + hd)),
            pl.BlockSpec((seq, DIFF_V), lambda b, hd, qi: (b, OFF_CK // DIFF_V + hd)),
            pl.BlockSpec((DIFF_V, seq), lambda b, hd, qi: (hd, b)),
            pl.BlockSpec((tq, DIFF_V), lambda b, hd, qi: (b * nq + qi, OFF_CG // DIFF_V + hd)),
            pl.BlockSpec((1, DIFF_V), lambda b, hd, qi: (0, 0)),
        ],
        out_specs=pl.BlockSpec((tq, DIFF_V), lambda b, hd, qi: (b * nq + qi, hd)),
        out_shape=jax.ShapeDtypeStruct((t, WIDTH_C), jnp.bfloat16),
        compiler_params=_vmem_params(block_bytes, 6 * tq * seq * 4),
        name="attn_c",
    )(lam_q1, lam_k1, lam_q2, lam_k2, h, h, vt, h, diff_subln)


def _na_bias_table(rpb, rows):
    n_groups = rows // NA_GROUP_ROWS
    kh = NA_WIN_ROWS
    a = np.arange(NA_GROUP_ROWS)[:, None]
    i = np.arange(NA_KEY_ROWS)[None, :]
    dr_list, row_ok_list = [], []
    for grp in (0, 1, n_groups - 1):
        k_start = NA_GROUP_ROWS * min(max(grp - 1, 0), n_groups - 3)
        r = NA_GROUP_ROWS * grp + a
        row_start = np.clip(r - kh // 2, 0, rows - kh)
        k_row = k_start + i
        row_ok_list.append((k_row >= row_start) & (k_row < row_start + kh))
        dr_list.append(np.clip(k_row - r + (NA_WIN_ROWS - 1), 0, 2 * NA_WIN_ROWS - 2))
    dr = np.stack(dr_list)
    row_ok = np.stack(row_ok_list)
    c = np.arange(GRID_W)
    col_start = np.clip(c - NA_WIN_COLS // 2, 0, GRID_W - NA_WIN_COLS)
    col_ok = (c[None, :] >= col_start[:, None]) & (c[None, :] < col_start[:, None] + NA_WIN_COLS)
    dc = np.clip(c[None, :] - c[:, None] + (NA_WIN_COLS - 1), 0, 2 * NA_WIN_COLS - 2)
    ok = row_ok[:, None, :, None, :, None] & col_ok[None, None, None, :, None, :]
    pick_row = (dr[..., None] == np.arange(2 * NA_WIN_ROWS - 1)).astype(np.float32)
    pick_col = (dc[..., None] == np.arange(2 * NA_WIN_COLS - 1)).astype(np.float32)
    bias = jnp.einsum('vair,hrc,qkc->vhaqik', pick_row, rpb.astype(jnp.float32) * LOG2E, pick_col,
                      precision=lax.Precision.HIGHEST)
    bias = jnp.where(ok, bias, NEG_BIG)
    return bias.reshape(3, NA_HEADS, NA_GROUP_ROWS * GRID_W, NA_KEY_ROWS * GRID_W)


def _attn_a_kernel(q_ref, k_ref, v_ref, g_ref, bias_ref, o_ref, *, n_groups):
    j = pl.program_id(2)
    gq = q_ref.shape[0]
    gk = bias_ref.shape[-1]
    start = pl.multiple_of(jnp.clip(j - 1, 0, n_groups - 3) * gq, gq)
    p, rl = _softmax_exp2(_qk_scores(q_ref[...], k_ref[pl.ds(start, gk), :]) + bias_ref[0, 0])
    o = jnp.dot(p.astype(jnp.bfloat16), v_ref[pl.ds(start, gk), :], preferred_element_type=jnp.float32)
    o_ref[...] = (o * rl * g_ref[...].astype(jnp.float32)).astype(o_ref.dtype)


def _attn_a(h, bias, batch, seq):
    t = h.shape[0]
    gq = NA_GROUP_ROWS * GRID_W
    gk = NA_KEY_ROWS * GRID_W
    n_groups = seq // gq
    assert n_groups >= 3

    def variant(j):
        return jnp.where(j == 0, 0, jnp.where(j == n_groups - 1, 2, 1))

    block_bytes = (3 * gq * HEAD_DIM + 2 * seq * HEAD_DIM) * 2 + gq * gk * 4
    return pl.pallas_call(
        functools.partial(_attn_a_kernel, n_groups=n_groups),
        grid=(batch, NA_HEADS, n_groups),
        in_specs=[
            pl.BlockSpec((gq, HEAD_DIM), lambda b, hd, j: (b * n_groups + j, OFF_AQ // HEAD_DIM + hd)),
            pl.BlockSpec((seq, HEAD_DIM), lambda b, hd, j: (b, OFF_AK // HEAD_DIM + hd)),
            pl.BlockSpec((seq, HEAD_DIM), lambda b, hd, j: (b, OFF_AV // HEAD_DIM + hd)),
            pl.BlockSpec((gq, HEAD_DIM), lambda b, hd, j: (b * n_groups + j, OFF_AG // HEAD_DIM + hd)),
            pl.BlockSpec((1, 1, gq, gk), lambda b, hd, j: (variant(j), hd, 0, 0)),
        ],
        out_specs=pl.BlockSpec((gq, HEAD_DIM), lambda b, hd, j: (b * n_groups + j, hd)),
        out_shape=jax.ShapeDtypeStruct((t, WIDTH_A), jnp.bfloat16),
        compiler_params=_vmem_params(block_bytes, 4 * gq * gk * 4),
        name="attn_a",
    )(h, h, h, h, bias)


def _merge_kernel(ya_ref, yb_ref, yc_ref, wa_ref, wb_ref, wc_ref, ga_ref, gb_ref, gc_ref, o_ref):
    def branch(y_ref, w_ref, g_ref):
        return g_ref[...].astype(jnp.float32) * jnp.dot(y_ref[...], w_ref[...], preferred_element_type=jnp.float32)
    merged = branch(ya_ref, wa_ref, ga_ref) + branch(yb_ref, wb_ref, gb_ref) + branch(yc_ref, wc_ref, gc_ref)
    o_ref[...] = merged.astype(o_ref.dtype)


def _merge(ya, yb, yc, w_o_a, w_o_b, w_o_c, h, d_model):
    t = ya.shape[0]
    tm = _tile(t, 512, 8)
    tn = _tile(d_model, 1024, LANE)
    y_spec = pl.BlockSpec((tm, WIDTH_A), lambda i, j: (i, 0))
    w_spec = pl.BlockSpec((WIDTH_A, tn), lambda i, j: (0, j))

    def gate_spec(branch):
        off = (OFF_MERGE + branch * d_model) // tn
        return pl.BlockSpec((tm, tn), lambda i, j: (i, off + j))

    assert OFF_MERGE % tn == 0 and d_model % tn == 0
    block_bytes = (3 * tm * WIDTH_A + 3 * WIDTH_A * tn + 4 * tm * tn) * 2
    return pl.pallas_call(
        _merge_kernel,
        grid=(t // tm, d_model // tn),
        in_specs=[y_spec, y_spec, y_spec, w_spec, w_spec, w_spec, gate_spec(0), gate_spec(1), gate_spec(2)],
        out_specs=pl.BlockSpec((tm, tn), lambda i, j: (i, j)),
        out_shape=jax.ShapeDtypeStruct((t, d_model), jnp.bfloat16),
        compiler_params=_vmem_params(block_bytes, 4 * tm * tn * 4),
        name="merge",
    )(ya, yb, yc, w_o_a, w_o_b, w_o_c, h, h, h)


def _out_kernel(m_ref, w_ref, x_ref, o_ref, *, alpha):
    y = jnp.dot(m_ref[...], w_ref[...], preferred_element_type=jnp.float32)
    o_ref[...] = alpha * x_ref[...] + y


def _out_proj(merged, w_out, x, alpha):
    t, d = x.shape
    tm = _tile(t, 512, 8)
    tn = _tile(d, 1024, LANE)
    block_bytes = (tm * d + d * tn) * 2 + 2 * tm * tn * 4
    return pl.pallas_call(
        functools.partial(_out_kernel, alpha=alpha),
        grid=(t // tm, d // tn),
        in_specs=[
            pl.BlockSpec((tm, d), lambda i, j: (i, 0)),
            pl.BlockSpec((d, tn), lambda i, j: (0, j)),
            pl.BlockSpec((tm, tn), lambda i, j: (i, j)),
        ],
        out_specs=pl.BlockSpec((tm, tn), lambda i, j: (i, j)),
        out_shape=jax.ShapeDtypeStruct((t, d), jnp.float32),
        compiler_params=_vmem_params(block_bytes, 2 * tm * tn * 4),
        name="out_proj",
    )(merged, w_out, x)


def _ln_kernel(z_ref, g_ref, b_ref, o_ref, obf_ref):
    z = z_ref[...]
    mu = jnp.mean(z, axis=-1, keepdims=True)
    zc = z - mu
    var = jnp.mean(zc * zc, axis=-1, keepdims=True)
    out = zc * lax.rsqrt(var + LN_EPS) * g_ref[...] + b_ref[...]
    o_ref[...] = out
    obf_ref[...] = out.astype(obf_ref.dtype)


def _layer_norm(z, ln_g, ln_b):
    t, d = z.shape
    tm = _tile(t, 256, 8)
    row = pl.BlockSpec((tm, d), lambda i: (i, 0))
    vec = pl.BlockSpec((1, d), lambda i: (0, 0))
    return pl.pallas_call(
        _ln_kernel,
        grid=(t // tm,),
        in_specs=[row, vec, vec],
        out_specs=[row, row],
        out_shape=[jax.ShapeDtypeStruct((t, d), jnp.float32), jax.ShapeDtypeStruct((t, d), jnp.bfloat16)],
        compiler_params=_vmem_params(tm * d * 10, 3 * tm * d * 4),
        name="layer_norm",
    )(z, ln_g, ln_b)


def _rope_tables(seq):
    half = MLA_ROPE // 2
    inv_freq = ROPE_THETA ** (-jnp.arange(half, dtype=jnp.float32) * 2.0 / MLA_ROPE)
    ang = jnp.arange(seq, dtype=jnp.float32)[:, None] * inv_freq[None, :]
    cos = jnp.cos(ang)
    sin = jnp.sin(ang)
    cos_t = jnp.concatenate([cos, cos, cos, cos], axis=-1)
    sin_t = jnp.concatenate([-sin, sin, -sin, sin], axis=-1)
    return cos_t, sin_t


def _in_weight_layout(w_in_l, d_model):
    splits = (WIDTH_A, WIDTH_A, WIDTH_A, WIDTH_A, MLA_Q_RANK, MLA_KV_RANK, MLA_ROPE, WIDTH_B,
              WIDTH_C, WIDTH_C, WIDTH_C, WIDTH_C, N_BRANCH * d_model)
    points = np.cumsum(np.array(splits))[:-1].tolist()
    (a_q, a_k, a_v, a_g, b_cq, b_ckv, b_kr, b_g, c_q, c_k, c_v, c_g, mg) = jnp.split(w_in_l, points, axis=1)
    main = jnp.concatenate([b_cq, b_ckv, a_q, a_k, a_v, a_g, b_g, c_q, c_k, c_v, c_g, mg], axis=1)
    kr = jnp.pad(b_kr, ((0, 0), (0, LANE - MLA_ROPE)))
    return main.astype(jnp.bfloat16), kr.astype(jnp.bfloat16)


def _layer(x, x_bf, layer_idx, depth, batch, seq, cos_t, sin_t, w_in, w_uq, q_norm, w_ukv, kv_norm, na_rpb,
           lam_q1, lam_k1, lam_q2, lam_k2, diff_subln, w_o_a, w_o_b, w_o_c, b_merge, w_out, ln_g, ln_b):
    t, d_model = x.shape
    f32 = jnp.float32
    bf16 = jnp.bfloat16
    tn = 1024
    assert OFF_MERGE % tn == 0 and d_model % tn == 0 and WIDTH_C == tn

    w_main, w_kr = _in_weight_layout(w_in, d_model)
    n_main = w_main.shape[1]
    nj = n_main // tn
    blk = lambda off: off // tn
    mode_ranges = (
        (MODE_SCALE, 0, blk(OFF_AG)),
        (MODE_SILU, blk(OFF_AG), blk(OFF_CQ)),
        (MODE_ROPE, blk(OFF_CQ), blk(OFF_CV)),
        (MODE_SCALE, blk(OFF_CV), blk(OFF_CG)),
        (MODE_SILU, blk(OFF_CG), blk(OFF_MERGE)),
        (MODE_SIGMOID, blk(OFF_MERGE), nj),
    )
    scales = np.ones((nj,), np.float32)
    scales[blk(OFF_AQ)] = HEAD_DIM ** -0.5 * LOG2E
    scales[blk(OFF_CQ)] = DIFF_QK ** -0.5 * LOG2E
    bias = jnp.concatenate([jnp.zeros((OFF_MERGE,), f32), b_merge.astype(f32)])[None, :]
    h, vt_c = _project(x_bf, w_main, bias, jnp.asarray(scales), mode_ranges, cos_t, sin_t, seq, 1024, tn,
                       t_block=blk(OFF_CV))

    k_rope, = _project(x_bf, w_kr, jnp.zeros((1, LANE), f32), jnp.ones((1,), f32),
                       ((MODE_ROPE, 0, 1),), cos_t, sin_t, seq, 1024, LANE)

    w_uq_pad = jnp.pad(w_uq.reshape(MLA_Q_RANK, MLA_HEADS, MLA_NOPE + MLA_ROPE),
                       ((0, 0), (0, 0), (0, MLA_QK_PAD - MLA_NOPE - MLA_ROPE)))
    w_uq_pad = w_uq_pad.reshape(MLA_Q_RANK, MLA_HEADS * MLA_QK_PAD).astype(bf16)
    w_ukv_r = w_ukv.reshape(MLA_KV_RANK, MLA_HEADS, MLA_NOPE + MLA_V)
    w_ukv_r = jnp.concatenate([w_ukv_r[:, :, :MLA_NOPE].reshape(MLA_KV_RANK, -1),
                               w_ukv_r[:, :, MLA_NOPE:].reshape(MLA_KV_RANK, -1)], axis=1).astype(bf16)
    q_pad = _mla_q(h, q_norm.astype(f32)[None, :], w_uq_pad, cos_t, sin_t, seq)
    k_pad, vt_b = _mla_kv(h, kv_norm.astype(f32)[None, :], w_ukv_r, k_rope)
    yb = _attn_b(q_pad, k_pad, vt_b, h, batch, seq)

    yc = _attn_c(h, vt_c, lam_q1.astype(f32)[None, :], lam_k1.astype(f32)[None, :], lam_q2.astype(f32)[None, :],
                 lam_k2.astype(f32)[None, :], diff_subln.astype(f32)[None, :], layer_idx, batch, seq)

    ya = _attn_a(h, _na_bias_table(na_rpb, seq // GRID_W), batch, seq)

    merged = _merge(ya, yb, yc, w_o_a.astype(bf16), w_o_b.astype(bf16), w_o_c.astype(bf16), h, d_model)
    z = _out_proj(merged, w_out.astype(bf16), x, (2.0 * depth) ** 0.25)
    return _layer_norm(z, ln_g.astype(f32)[None, :], ln_b.astype(f32)[None, :])


def kernel(x, w_in, w_uq, q_norm, w_ukv, kv_norm, na_rpb, lam_q1, lam_k1, lam_q2, lam_k2, diff_subln,
           w_o_a, w_o_b, w_o_c, b_merge, w_out, ln_g, ln_b):
    batch, seq, d_model = x.shape
    depth = w_in.shape[0]
    assert seq % (NA_GROUP_ROWS * GRID_W) == 0 and seq // GRID_W >= NA_WIN_ROWS + NA_GROUP_ROWS
    cos_t, sin_t = _rope_tables(seq)
    xf = x.reshape(batch * seq, d_model)
    x_bf = xf.astype(jnp.bfloat16)
    for l in range(depth):
        xf, x_bf = _layer(xf, x_bf, l, depth, batch, seq, cos_t, sin_t, w_in[l], w_uq[l], q_norm[l], w_ukv[l],
                          kv_norm[l], na_rpb[l], lam_q1[l], lam_k1[l], lam_q2[l], lam_k2[l], diff_subln[l],
                          w_o_a[l], w_o_b[l], w_o_c[l], b_merge[l], w_out[l], ln_g[l], ln_b[l])
    return xf.reshape(batch, seq, d_model)
```

```python
import functools
import math

import jax
import jax.numpy as jnp
import numpy as np
from jax import lax
from jax.experimental import pallas as pl
from jax.experimental.pallas import tpu as pltpu

GRID_W = 64
HEAD_DIM = 128
ROPE_THETA = 10000.0
LN_EPS = 1e-5
RMS_EPS = 1e-6

NA_HEADS = 8
NA_WIN_ROWS = 8
NA_WIN_COLS = 16
WIDTH_A = NA_HEADS * HEAD_DIM

MLA_HEADS = 8
MLA_Q_RANK = 1536
MLA_KV_RANK = 512
MLA_NOPE = 128
MLA_ROPE = 64
MLA_V = 128
WIDTH_B = MLA_HEADS * MLA_V

DIFF_HEADS = 8
DIFF_QK = 64
DIFF_V = 2 * DIFF_QK
WIDTH_C = DIFF_HEADS * DIFF_V

N_BRANCH = 3

LANE = 128
MXU_WIDTH = 256
MLA_QK_PAD = 2 * LANE
NEG_BIG = -1e30
LOG2E = math.log2(math.e)
VMEM_CAP = 60 * 1024 * 1024

H1_AQ = 0
H1_AK = H1_AQ + WIDTH_A
H1_AV = H1_AK + WIDTH_A
H1_AG = H1_AV + WIDTH_A
H1_BCQ = H1_AG + WIDTH_A
H1_BCKV = H1_BCQ + MLA_Q_RANK
H1_WIDTH = H1_BCKV + MLA_KV_RANK
H2_BG = 0
H2_CQ = H2_BG + WIDTH_B
H2_CK = H2_CQ + WIDTH_C
H2_CV = H2_CK + WIDTH_C
H2_CG = H2_CV + WIDTH_C
H2_MERGE = H2_CG + WIDTH_C
PROJ_TN = 1024
MLA_IN_BLOCK = MLA_Q_RANK + MLA_KV_RANK

ATTN_CHAIN = 2 * MXU_WIDTH
NA_GROUP_ROWS = 4
NA_KEY_ROWS = NA_GROUP_ROWS + NA_WIN_ROWS
SLAB_ROWS = 256


def _vmem_params(block_bytes, temp_bytes=0):
    need = 2 * block_bytes + temp_bytes + (4 << 20)
    return pltpu.CompilerParams(vmem_limit_bytes=int(min(max(need, 16 << 20), VMEM_CAP)))


def _tile(n, pref, quantum):
    t = min(pref, n)
    t -= t % quantum
    while t > quantum and n % t:
        t -= quantum
    assert t >= quantum and n % t == 0, (n, pref, quantum)
    return t


def _rope_slab(xs, cos, sin_signed):
    half = MLA_ROPE // 2
    lane = lax.broadcasted_iota(jnp.int32, xs.shape, 1)
    first_half = (lane % MLA_ROPE) < half
    partner = jnp.where(first_half, pltpu.roll(xs, LANE - half, 1), pltpu.roll(xs, half, 1))
    return xs * cos + partner * sin_signed


def _proj_kernel(coef_ref, x_ref, w_ref, b_ref, cos_ref, sin_ref, o_ref, *ot_ref,
                 silu_ranges, sigmoid_range, rope_range, t_block, m_slab):
    j = pl.program_id(1)
    c1 = coef_ref[j]

    def in_range(r):
        return (j >= r[0]) & (j < r[1])

    is_rope = in_range(rope_range)
    is_t = (j == t_block) if t_block is not None else (j < 0)
    is_silu = functools.reduce(jnp.logical_or, [in_range(r) for r in silu_ranges], j < 0)
    is_sigmoid = in_range(sigmoid_range)

    def slabbed(epilogue):
        for m in range(x_ref.shape[0] // m_slab):
            rows = pl.ds(m * m_slab, m_slab)
            acc = jnp.dot(x_ref[rows, :], w_ref[...], preferred_element_type=jnp.float32)
            o_ref[rows, :] = epilogue(acc).astype(o_ref.dtype)

    @pl.when(is_silu)
    def _():
        slabbed(lambda acc: acc * jax.nn.sigmoid(acc))

    @pl.when(is_sigmoid)
    def _():
        bias = b_ref[...]
        slabbed(lambda acc: jax.nn.sigmoid(acc + bias))

    @pl.when(jnp.logical_not(is_silu | is_sigmoid | is_rope | is_t))
    def _():
        slabbed(lambda acc: acc * c1)

    @pl.when(is_rope)
    def _():
        acc = jnp.dot(x_ref[...], w_ref[...], preferred_element_type=jnp.float32)
        cos = cos_ref[...]
        sin = sin_ref[...]
        for s in range(acc.shape[1] // LANE):
            sl = slice(s * LANE, (s + 1) * LANE)
            o_ref[:, sl] = (_rope_slab(acc[:, sl], cos, sin) * c1).astype(o_ref.dtype)

    if t_block is not None:
        @pl.when(is_t)
        def _():
            acc = jnp.dot(x_ref[...], w_ref[...], preferred_element_type=jnp.float32)
            o_ref[...] = acc.astype(o_ref.dtype)
            ot_ref[0][...] = acc.T.astype(ot_ref[0].dtype)


def _project(x_bf, w_bf, bias, coefs, cos_t, sin_t, seq, layer, tn, *,
             silu_ranges=(), sigmoid_range=(0, 0), rope_range=(0, 0), t_block=None):
    t, k = x_bf.shape
    n = w_bf.shape[-1]
    tm = _tile(math.gcd(t, seq), 1024, 8)
    nj = n // tn
    pos_blocks = seq // tm
    out_specs = [pl.BlockSpec((tm, tn), lambda i, j: (i, j))]
    out_shape = [jax.ShapeDtypeStruct((t, n), jnp.bfloat16)]
    if t_block is not None:
        out_specs.append(pl.BlockSpec((tn, tm), lambda i, j: (0, i)))
        out_shape.append(jax.ShapeDtypeStruct((tn, t), jnp.bfloat16))
    block_bytes = (tm * k // 2 + k * tn + len(out_specs) * tm * tn) * 2 + tn * 4 + 2 * tm * LANE * 4
    return pl.pallas_call(
        functools.partial(_proj_kernel, silu_ranges=silu_ranges, sigmoid_range=sigmoid_range, rope_range=rope_range,
                          t_block=t_block, m_slab=_tile(tm, SLAB_ROWS, 8)),
        grid=(t // tm, nj),
        in_specs=[
            pl.BlockSpec(memory_space=pltpu.SMEM),
            pl.BlockSpec((tm, k), lambda i, j: (i, 0), pipeline_mode=pl.Buffered(1)),
            pl.BlockSpec((None, k, tn), lambda i, j: (layer, 0, j)),
            pl.BlockSpec((None, 1, tn), lambda i, j: (layer, 0, j)),
            pl.BlockSpec((tm, LANE), lambda i, j: (i % pos_blocks, 0)),
            pl.BlockSpec((tm, LANE), lambda i, j: (i % pos_blocks, 0)),
        ],
        out_specs=out_specs,
        out_shape=out_shape,
        compiler_params=_vmem_params(block_bytes, 3 * tm * tn * 4),
        name="project",
    )(coefs, x_bf, w_bf, bias, cos_t, sin_t)


def _rms_normed(c, g_ref):
    c = c.astype(jnp.float32)
    ms = jnp.mean(c * c, axis=-1, keepdims=True)
    return (c * lax.rsqrt(ms + RMS_EPS) * g_ref[...]).astype(jnp.bfloat16)


def _mla_q_kernel(h_ref, g_ref, w_ref, cos_ref, sin_ref, o_ref, *, scale):
    cn = _rms_normed(h_ref[:, :MLA_Q_RANK], g_ref)
    acc = jnp.dot(cn, w_ref[...], preferred_element_type=jnp.float32)
    cos = cos_ref[...]
    sin = sin_ref[...]
    for h in range(MLA_HEADS):
        base = h * MLA_QK_PAD
        o_ref[:, base:base + LANE] = (acc[:, base:base + LANE] * scale).astype(o_ref.dtype)
        rope = _rope_slab(acc[:, base + LANE:base + 2 * LANE], cos, sin)
        o_ref[:, base + LANE:base + 2 * LANE] = (rope * scale).astype(o_ref.dtype)


def _mla_q(h1, q_norm, w_uq_pad, cos_t, sin_t, seq, layer):
    t = h1.shape[0]
    tm = _tile(math.gcd(t, seq), 512, 8)
    pos_blocks = seq // tm
    n = MLA_HEADS * MLA_QK_PAD
    assert H1_BCQ % MLA_IN_BLOCK == 0
    block_bytes = (tm * MLA_IN_BLOCK + MLA_Q_RANK * n + tm * n) * 2 + 2 * tm * LANE * 4
    return pl.pallas_call(
        functools.partial(_mla_q_kernel, scale=(MLA_NOPE + MLA_ROPE) ** -0.5 * LOG2E),
        grid=(t // tm,),
        in_specs=[
            pl.BlockSpec((tm, MLA_IN_BLOCK), lambda i: (i, H1_BCQ // MLA_IN_BLOCK)),
            pl.BlockSpec((None, 1, MLA_Q_RANK), lambda i: (layer, 0, 0)),
            pl.BlockSpec((None, MLA_Q_RANK, n), lambda i: (layer, 0, 0)),
            pl.BlockSpec((tm, LANE), lambda i: (i % pos_blocks, 0)),
            pl.BlockSpec((tm, LANE), lambda i: (i % pos_blocks, 0)),
        ],
        out_specs=pl.BlockSpec((tm, n), lambda i: (i, 0)),
        out_shape=jax.ShapeDtypeStruct((t, n), jnp.bfloat16),
        compiler_params=_vmem_params(block_bytes, tm * (2 * n + 3 * MLA_Q_RANK) * 4),
        name="mla_q",
    )(h1, q_norm, w_uq_pad, cos_t, sin_t)


def _mla_kv_kernel(h_ref, g_ref, w_ref, kr_ref, k_ref, vt_ref):
    acc = jnp.dot(_rms_normed(h_ref[...], g_ref), w_ref[...], preferred_element_type=jnp.float32)
    kr = kr_ref[...]
    for h in range(MLA_HEADS):
        base = h * MLA_QK_PAD
        k_ref[:, base:base + LANE] = acc[:, h * MLA_NOPE:(h + 1) * MLA_NOPE].astype(k_ref.dtype)
        k_ref[:, base + LANE:base + 2 * LANE] = kr
    vt_ref[...] = acc[:, MLA_HEADS * MLA_NOPE:].T.astype(vt_ref.dtype)


def _mla_kv(h1, kv_norm, w_ukv_r, k_rope, layer):
    t = h1.shape[0]
    tm = _tile(t, 512, LANE)
    nk = MLA_HEADS * MLA_QK_PAD
    nw = MLA_HEADS * (MLA_NOPE + MLA_V)
    assert H1_BCKV % MLA_KV_RANK == 0
    block_bytes = (tm * MLA_KV_RANK + MLA_KV_RANK * nw + tm * LANE + tm * nk + tm * WIDTH_B) * 2
    return pl.pallas_call(
        _mla_kv_kernel,
        grid=(t // tm,),
        in_specs=[
            pl.BlockSpec((tm, MLA_KV_RANK), lambda i: (i, H1_BCKV // MLA_KV_RANK)),
            pl.BlockSpec((None, 1, MLA_KV_RANK), lambda i: (layer, 0, 0)),
            pl.BlockSpec((None, MLA_KV_RANK, nw), lambda i: (layer, 0, 0)),
            pl.BlockSpec((tm, LANE), lambda i: (i, 0)),
        ],
        out_specs=[
            pl.BlockSpec((tm, nk), lambda i: (i, 0)),
            pl.BlockSpec((WIDTH_B, tm), lambda i: (0, i)),
        ],
        out_shape=[
            jax.ShapeDtypeStruct((t, nk), jnp.bfloat16),
            jax.ShapeDtypeStruct((WIDTH_B, t), jnp.bfloat16),
        ],
        compiler_params=_vmem_params(block_bytes, tm * (2 * nw + 2 * MLA_KV_RANK) * 4),
        name="mla_kv",
    )(h1, kv_norm, w_ukv_r, k_rope)


def _qk_scores(a, b):
    return lax.dot_general(a, b, (((1,), (1,)), ((), ())), preferred_element_type=jnp.float32)


def _softmax_exp2(s):
    p = jnp.exp2(s - jnp.max(s, axis=-1, keepdims=True))
    return p, 1.0 / jnp.sum(p, axis=-1, keepdims=True)


def _keys_major_attention(k, q, vt):
    st = _qk_scores(k, q)
    e = jnp.exp2(st - jnp.max(st, axis=0, keepdims=True))
    rl = 1.0 / jnp.sum(e, axis=0, keepdims=True)
    return jnp.dot(vt, e.astype(jnp.bfloat16), preferred_element_type=jnp.float32) * rl


def _attn_b_kernel(q_ref, k_ref, vt_ref, g_ref, o_ref):
    for c in range(q_ref.shape[0] // ATTN_CHAIN):
        rows = pl.ds(c * ATTN_CHAIN, ATTN_CHAIN)
        ot = _keys_major_attention(k_ref[...], q_ref[rows, :], vt_ref[...])
        o_ref[rows, :] = (ot.T * g_ref[rows, :].astype(jnp.float32)).astype(o_ref.dtype)


def _attn_b(q_pad, k_pad, vt, h2, batch, seq):
    t = q_pad.shape[0]
    tq = _tile(seq, 1024, ATTN_CHAIN)
    nq = seq // tq
    block_bytes = (tq * MLA_QK_PAD + seq * MLA_QK_PAD + seq * MLA_V + 2 * tq * MLA_V) * 2
    return pl.pallas_call(
        _attn_b_kernel,
        grid=(batch, MLA_HEADS, nq),
        in_specs=[
            pl.BlockSpec((tq, MLA_QK_PAD), lambda b, hd, qi: (b * nq + qi, hd)),
            pl.BlockSpec((seq, MLA_QK_PAD), lambda b, hd, qi: (b, hd)),
            pl.BlockSpec((MLA_V, seq), lambda b, hd, qi: (hd, b)),
            pl.BlockSpec((tq, MLA_V), lambda b, hd, qi: (b * nq + qi, H2_BG // MLA_V + hd)),
        ],
        out_specs=pl.BlockSpec((tq, MLA_V), lambda b, hd, qi: (b * nq + qi, hd)),
        out_shape=jax.ShapeDtypeStruct((t, WIDTH_B), jnp.bfloat16),
        compiler_params=_vmem_params(block_bytes, 3 * tq * seq * 4),
        name="attn_b",
    )(q_pad, k_pad, vt, h2)


def _attn_c_kernel(lq1_ref, lk1_ref, lq2_ref, lk2_ref, q_ref, k_ref, vt_ref, g_ref, sub_ref, o_ref, *, lam_init):
    lam = (jnp.exp(jnp.sum(lq1_ref[...] * lk1_ref[...], keepdims=True))
           - jnp.exp(jnp.sum(lq2_ref[...] * lk2_ref[...], keepdims=True)) + lam_init)
    for c in range(q_ref.shape[0] // ATTN_CHAIN):
        rows = pl.ds(c * ATTN_CHAIN, ATTN_CHAIN)
        q = q_ref[rows, :]
        lane = lax.broadcasted_iota(jnp.int32, q.shape, 1)
        zero = jnp.zeros_like(q)
        ot1 = _keys_major_attention(k_ref[...], jnp.where(lane < DIFF_QK, q, zero), vt_ref[...])
        ot2 = _keys_major_attention(k_ref[...], jnp.where(lane >= DIFF_QK, q, zero), vt_ref[...])
        ot = ot1 - lam * ot2
        ms = jnp.mean(ot * ot, axis=0, keepdims=True)
        o = (ot * lax.rsqrt(ms + RMS_EPS)).T * sub_ref[...] * (1.0 - lam_init)
        o_ref[rows, :] = (o * g_ref[rows, :].astype(jnp.float32)).astype(o_ref.dtype)


def _attn_c(h2, vt, lam_q1, lam_k1, lam_q2, lam_k2, diff_subln, layer, batch, seq):
    t = h2.shape[0]
    tq = _tile(seq, 1024, ATTN_CHAIN)
    nq = seq // tq
    lam_init = 0.8 - 0.6 * math.exp(-0.3 * layer)
    lam_spec = pl.BlockSpec((None, 1, DIFF_QK), lambda b, hd, qi: (layer, 0, 0))
    block_bytes = (3 * tq * DIFF_V + 2 * seq * DIFF_V) * 2
    return pl.pallas_call(
        functools.partial(_attn_c_kernel, lam_init=lam_init),
        grid=(batch, DIFF_HEADS, nq),
        in_specs=[
            lam_spec, lam_spec, lam_spec, lam_spec,
            pl.BlockSpec((tq, DIFF_V), lambda b, hd, qi: (b * nq + qi, H2_CQ // DIFF_V + hd)),
            pl.BlockSpec((seq, DIFF_V), lambda b, hd, qi: (b, H2_CK // DIFF_V + hd)),
            pl.BlockSpec((DIFF_V, seq), lambda b, hd, qi: (hd, b)),
            pl.BlockSpec((tq, DIFF_V), lambda b, hd, qi: (b * nq + qi, H2_CG // DIFF_V + hd)),
            pl.BlockSpec((None, 1, DIFF_V), lambda b, hd, qi: (layer, 0, 0)),
        ],
        out_specs=pl.BlockSpec((tq, DIFF_V), lambda b, hd, qi: (b * nq + qi, hd)),
        out_shape=jax.ShapeDtypeStruct((t, WIDTH_C), jnp.bfloat16),
        compiler_params=_vmem_params(block_bytes, 6 * tq * seq * 4),
        name="attn_c",
    )(lam_q1, lam_k1, lam_q2, lam_k2, h2, h2, vt, h2, diff_subln)


def _na_bias_table(rpb, rows):
    n_groups = rows // NA_GROUP_ROWS
    kh = NA_WIN_ROWS
    a = np.arange(NA_GROUP_ROWS)[:, None]
    i = np.arange(NA_KEY_ROWS)[None, :]
    dr_list, row_ok_list = [], []
    for grp in (0, 1, n_groups - 1):
        k_start = NA_GROUP_ROWS * min(max(grp - 1, 0), n_groups - 3)
        r = NA_GROUP_ROWS * grp + a
        row_start = np.clip(r - kh // 2, 0, rows - kh)
        k_row = k_start + i
        row_ok_list.append((k_row >= row_start) & (k_row < row_start + kh))
        dr_list.append(np.clip(k_row - r + (NA_WIN_ROWS - 1), 0, 2 * NA_WIN_ROWS - 2))
    dr = np.stack(dr_list)
    row_ok = np.stack(row_ok_list)
    c = np.arange(GRID_W)
    col_start = np.clip(c - NA_WIN_COLS // 2, 0, GRID_W - NA_WIN_COLS)
    col_ok = (c[None, :] >= col_start[:, None]) & (c[None, :] < col_start[:, None] + NA_WIN_COLS)
    dc = np.clip(c[None, :] - c[:, None] + (NA_WIN_COLS - 1), 0, 2 * NA_WIN_COLS - 2)
    ok = row_ok[:, None, :, None, :, None] & col_ok[None, None, None, :, None, :]
    pick_row = (dr[..., None] == np.arange(2 * NA_WIN_ROWS - 1)).astype(np.float32)
    pick_col = (dc[..., None] == np.arange(2 * NA_WIN_COLS - 1)).astype(np.float32)
    bias = jnp.einsum('vair,lhrc,qkc->lvhaqik', pick_row, rpb.astype(jnp.float32) * LOG2E, pick_col,
                      precision=lax.Precision.HIGHEST)
    bias = jnp.where(ok[None], bias, NEG_BIG)
    return bias.reshape(rpb.shape[0], 3, NA_HEADS, NA_GROUP_ROWS * GRID_W, NA_KEY_ROWS * GRID_W)


def _attn_a_kernel(q_ref, k_ref, v_ref, g_ref, bias_ref, o_ref, *, n_groups):
    j = pl.program_id(2)
    gq = q_ref.shape[0]
    gk = bias_ref.shape[-1]
    start = pl.multiple_of(jnp.clip(j - 1, 0, n_groups - 3) * gq, gq)
    p, rl = _softmax_exp2(_qk_scores(q_ref[...], k_ref[pl.ds(start, gk), :]) + bias_ref[...])
    o = jnp.dot(p.astype(jnp.bfloat16), v_ref[pl.ds(start, gk), :], preferred_element_type=jnp.float32)
    o_ref[...] = (o * rl * g_ref[...].astype(jnp.float32)).astype(o_ref.dtype)


def _attn_a(h1, bias, layer, batch, seq):
    t = h1.shape[0]
    gq = NA_GROUP_ROWS * GRID_W
    gk = NA_KEY_ROWS * GRID_W
    n_groups = seq // gq
    assert n_groups >= 3

    def variant(j):
        return jnp.where(j == 0, 0, jnp.where(j == n_groups - 1, 2, 1))

    block_bytes = (3 * gq * HEAD_DIM + 2 * seq * HEAD_DIM) * 2 + gq * gk * 4
    return pl.pallas_call(
        functools.partial(_attn_a_kernel, n_groups=n_groups),
        grid=(batch, NA_HEADS, n_groups),
        in_specs=[
            pl.BlockSpec((gq, HEAD_DIM), lambda b, hd, j: (b * n_groups + j, H1_AQ // HEAD_DIM + hd)),
            pl.BlockSpec((seq, HEAD_DIM), lambda b, hd, j: (b, H1_AK // HEAD_DIM + hd)),
            pl.BlockSpec((seq, HEAD_DIM), lambda b, hd, j: (b, H1_AV // HEAD_DIM + hd)),
            pl.BlockSpec((gq, HEAD_DIM), lambda b, hd, j: (b * n_groups + j, H1_AG // HEAD_DIM + hd)),
            pl.BlockSpec((None, None, None, gq, gk), lambda b, hd, j: (layer, variant(j), hd, 0, 0)),
        ],
        out_specs=pl.BlockSpec((gq, HEAD_DIM), lambda b, hd, j: (b * n_groups + j, hd)),
        out_shape=jax.ShapeDtypeStruct((t, WIDTH_A), jnp.bfloat16),
        compiler_params=_vmem_params(block_bytes, 4 * gq * gk * 4),
        name="attn_a",
    )(h1, h1, h1, h1, bias)


def _merge_kernel(ya_ref, yb_ref, yc_ref, wa_ref, wb_ref, wc_ref, ga_ref, gb_ref, gc_ref, o_ref, *, m_slab):
    for m in range(o_ref.shape[0] // m_slab):
        rows = pl.ds(m * m_slab, m_slab)

        def branch(y_ref, w_ref, g_ref):
            y = jnp.dot(y_ref[rows, :], w_ref[...], preferred_element_type=jnp.float32)
            return g_ref[rows, :].astype(jnp.float32) * y

        merged = branch(ya_ref, wa_ref, ga_ref) + branch(yb_ref, wb_ref, gb_ref) + branch(yc_ref, wc_ref, gc_ref)
        o_ref[rows, :] = merged.astype(o_ref.dtype)


def _merge(ya, yb, yc, w_o_a, w_o_b, w_o_c, h2, d_model, layer):
    t = ya.shape[0]
    tm = _tile(t, 512, 8)
    tn = _tile(d_model, 1024, LANE)
    y_spec = pl.BlockSpec((tm, WIDTH_A), lambda i, j: (i, 0))
    w_spec = pl.BlockSpec((None, WIDTH_A, tn), lambda i, j: (layer, 0, j))

    def gate_spec(branch):
        off = (H2_MERGE + branch * d_model) // tn
        return pl.BlockSpec((tm, tn), lambda i, j: (i, off + j))

    assert H2_MERGE % tn == 0 and d_model % tn == 0
    block_bytes = (3 * tm * WIDTH_A + 3 * WIDTH_A * tn + 4 * tm * tn) * 2
    return pl.pallas_call(
        functools.partial(_merge_kernel, m_slab=_tile(tm, SLAB_ROWS, 8)),
        grid=(t // tm, d_model // tn),
        in_specs=[y_spec, y_spec, y_spec, w_spec, w_spec, w_spec, gate_spec(0), gate_spec(1), gate_spec(2)],
        out_specs=pl.BlockSpec((tm, tn), lambda i, j: (i, j)),
        out_shape=jax.ShapeDtypeStruct((t, d_model), jnp.bfloat16),
        compiler_params=_vmem_params(block_bytes, 4 * tm * tn * 4),
        name="merge",
    )(ya, yb, yc, w_o_a, w_o_b, w_o_c, h2, h2, h2)


def _out_kernel(m_ref, w_ref, x_ref, o_ref, *, alpha, m_slab):
    for m in range(o_ref.shape[0] // m_slab):
        rows = pl.ds(m * m_slab, m_slab)
        y = jnp.dot(m_ref[rows, :], w_ref[...], preferred_element_type=jnp.float32)
        o_ref[rows, :] = alpha * x_ref[rows, :] + y


def _out_proj(merged, w_out, x, alpha, layer):
    t, d = x.shape
    tm = _tile(t, 512, 8)
    tn = _tile(d, 1024, LANE)
    block_bytes = (tm * d + d * tn) * 2 + 2 * tm * tn * 4
    return pl.pallas_call(
        functools.partial(_out_kernel, alpha=alpha, m_slab=_tile(tm, SLAB_ROWS, 8)),
        grid=(t // tm, d // tn),
        in_specs=[
            pl.BlockSpec((tm, d), lambda i, j: (i, 0)),
            pl.BlockSpec((None, d, tn), lambda i, j: (layer, 0, j)),
            pl.BlockSpec((tm, tn), lambda i, j: (i, j)),
        ],
        out_specs=pl.BlockSpec((tm, tn), lambda i, j: (i, j)),
        out_shape=jax.ShapeDtypeStruct((t, d), jnp.float32),
        compiler_params=_vmem_params(block_bytes, 2 * tm * tn * 4),
        name="out_proj",
    )(merged, w_out, x)


def _ln_kernel(z_ref, g_ref, b_ref, o_ref, obf_ref):
    z = z_ref[...]
    mu = jnp.mean(z, axis=-1, keepdims=True)
    zc = z - mu
    var = jnp.mean(zc * zc, axis=-1, keepdims=True)
    out = zc * lax.rsqrt(var + LN_EPS) * g_ref[...] + b_ref[...]
    o_ref[...] = out
    obf_ref[...] = out.astype(obf_ref.dtype)


def _layer_norm(z, ln_g, ln_b, layer):
    t, d = z.shape
    tm = _tile(t, 256, 8)
    row = pl.BlockSpec((tm, d), lambda i: (i, 0))
    vec = pl.BlockSpec((None, 1, d), lambda i: (layer, 0, 0))
    return pl.pallas_call(
        _ln_kernel,
        grid=(t // tm,),
        in_specs=[row, vec, vec],
        out_specs=[row, row],
        out_shape=[jax.ShapeDtypeStruct((t, d), jnp.float32), jax.ShapeDtypeStruct((t, d), jnp.bfloat16)],
        compiler_params=_vmem_params(tm * d * 10, 3 * tm * d * 4),
        name="layer_norm",
    )(z, ln_g, ln_b)


def _rope_tables(seq):
    half = MLA_ROPE // 2
    inv_freq = ROPE_THETA ** (-jnp.arange(half, dtype=jnp.float32) * 2.0 / MLA_ROPE)
    ang = jnp.arange(seq, dtype=jnp.float32)[:, None] * inv_freq[None, :]
    cos = jnp.cos(ang)
    sin = jnp.sin(ang)
    cos_t = jnp.concatenate([cos, cos, cos, cos], axis=-1)
    sin_t = jnp.concatenate([-sin, sin, -sin, sin], axis=-1)
    return cos_t, sin_t


def _row3(v):
    return v.astype(jnp.float32)[:, None, :]


def kernel(x, w_in, w_uq, q_norm, w_ukv, kv_norm, na_rpb, lam_q1, lam_k1, lam_q2, lam_k2, diff_subln,
           w_o_a, w_o_b, w_o_c, b_merge, w_out, ln_g, ln_b):
    batch, seq, d_model = x.shape
    depth = w_in.shape[0]
    t = batch * seq
    f32, bf16 = jnp.float32, jnp.bfloat16
    tn = PROJ_TN
    assert seq % (NA_GROUP_ROWS * GRID_W) == 0 and seq // GRID_W >= NA_WIN_ROWS + NA_GROUP_ROWS
    assert H1_WIDTH % tn == 0 and H2_MERGE % tn == 0 and d_model % tn == 0 and WIDTH_C == tn
    blk = lambda off: off // tn

    kr_lo, kr_hi = H1_WIDTH, H1_WIDTH + MLA_ROPE
    w1 = w_in[:, :, :kr_lo].astype(bf16)
    w_kr = jnp.pad(w_in[:, :, kr_lo:kr_hi], ((0, 0), (0, 0), (0, LANE - MLA_ROPE))).astype(bf16)
    w2 = w_in[:, :, kr_hi:].astype(bf16)
    n1, n2 = w1.shape[-1], w2.shape[-1]
    w_uq_pad = jnp.pad(w_uq.reshape(depth, MLA_Q_RANK, MLA_HEADS, MLA_NOPE + MLA_ROPE),
                       ((0, 0), (0, 0), (0, 0), (0, MLA_QK_PAD - MLA_NOPE - MLA_ROPE)))
    w_uq_pad = w_uq_pad.reshape(depth, MLA_Q_RANK, MLA_HEADS * MLA_QK_PAD).astype(bf16)
    w_ukv_h = w_ukv.reshape(depth, MLA_KV_RANK, MLA_HEADS, MLA_NOPE + MLA_V)
    w_ukv_r = jnp.concatenate([w_ukv_h[..., :MLA_NOPE].reshape(depth, MLA_KV_RANK, -1),
                               w_ukv_h[..., MLA_NOPE:].reshape(depth, MLA_KV_RANK, -1)], axis=-1).astype(bf16)
    w_oa, w_ob, w_oc, w_o = (w.astype(bf16) for w in (w_o_a, w_o_b, w_o_c, w_out))
    bias1 = jnp.zeros((depth, 1, n1), f32)
    bias2 = jnp.concatenate([jnp.zeros((depth, H2_MERGE), f32), b_merge.astype(f32)], axis=1)[:, None, :]
    bias_kr = jnp.zeros((depth, 1, LANE), f32)
    coef1 = np.ones((n1 // tn,), np.float32)
    coef1[blk(H1_AQ)] = HEAD_DIM ** -0.5 * LOG2E
    coef2 = np.ones((n2 // tn,), np.float32)
    coef2[blk(H2_CQ)] = DIFF_QK ** -0.5 * LOG2E
    na_bias = _na_bias_table(na_rpb, seq // GRID_W)
    cos_t, sin_t = _rope_tables(seq)
    q_norm3, kv_norm3, subln3, ln_g3, ln_b3 = (_row3(v) for v in (q_norm, kv_norm, diff_subln, ln_g, ln_b))
    lams = [_row3(v) for v in (lam_q1, lam_k1, lam_q2, lam_k2)]
    alpha = (2.0 * depth) ** 0.25

    xf = x.reshape(t, d_model)
    x_bf = xf.astype(bf16)
    for l in range(depth):
        h1, = _project(x_bf, w1, bias1, jnp.asarray(coef1), cos_t, sin_t, seq, l, tn,
                       silu_ranges=((blk(H1_AG), blk(H1_BCQ)),))
        h2, vt_c = _project(x_bf, w2, bias2, jnp.asarray(coef2), cos_t, sin_t, seq, l, tn,
                            silu_ranges=((blk(H2_BG), blk(H2_CQ)), (blk(H2_CG), blk(H2_MERGE))),
                            sigmoid_range=(blk(H2_MERGE), n2 // tn), rope_range=(blk(H2_CQ), blk(H2_CV)),
                            t_block=blk(H2_CV))
        k_rope, = _project(x_bf, w_kr, bias_kr, jnp.ones((1,), f32), cos_t, sin_t, seq, l, LANE, rope_range=(0, 1))

        q_pad = _mla_q(h1, q_norm3, w_uq_pad, cos_t, sin_t, seq, l)
        k_pad, vt_b = _mla_kv(h1, kv_norm3, w_ukv_r, k_rope, l)
        yb = _attn_b(q_pad, k_pad, vt_b, h2, batch, seq)
        yc = _attn_c(h2, vt_c, *lams, subln3, l, batch, seq)
        ya = _attn_a(h1, na_bias, l, batch, seq)

        merged = _merge(ya, yb, yc, w_oa, w_ob, w_oc, h2, d_model, l)
        z = _out_proj(merged, w_o, xf, alpha, l)
        xf, x_bf = _layer_norm(z, ln_g3, ln_b3, l)
    return xf.reshape(batch, seq, d_model)
```

```python
import functools
import math

import jax
import jax.numpy as jnp
import numpy as np
from jax import lax
from jax.experimental import pallas as pl
from jax.experimental.pallas import tpu as pltpu

GRID_W = 64
HEAD_DIM = 128
ROPE_THETA = 10000.0
LN_EPS = 1e-5
RMS_EPS = 1e-6

NA_HEADS = 8
NA_WIN_ROWS = 8
NA_WIN_COLS = 16
WIDTH_A = NA_HEADS * HEAD_DIM

MLA_HEADS = 8
MLA_Q_RANK = 1536
MLA_KV_RANK = 512
MLA_NOPE = 128
MLA_ROPE = 64
MLA_V = 128
WIDTH_B = MLA_HEADS * MLA_V

DIFF_HEADS = 8
DIFF_QK = 64
DIFF_V = 2 * DIFF_QK
WIDTH_C = DIFF_HEADS * DIFF_V

N_BRANCH = 3

LANE = 128
MXU_WIDTH = 256
MLA_QK_PAD = 2 * LANE
NEG_BIG = -1e30
LOG2E = math.log2(math.e)
VMEM_CAP = 60 * 1024 * 1024

H1_AQ = 0
H1_AK = H1_AQ + WIDTH_A
H1_AV = H1_AK + WIDTH_A
H1_AG = H1_AV + WIDTH_A
H1_BCQ = H1_AG + WIDTH_A
H1_BCKV = H1_BCQ + MLA_Q_RANK
H1_WIDTH = H1_BCKV + MLA_KV_RANK
H2_BG = 0
H2_CQ = H2_BG + WIDTH_B
H2_CK = H2_CQ + WIDTH_C
H2_CV = H2_CK + WIDTH_C
H2_CG = H2_CV + WIDTH_C
H2_MERGE = H2_CG + WIDTH_C
PROJ_TN = 1024
MLA_IN_BLOCK = MLA_Q_RANK + MLA_KV_RANK

ATTN_CHAIN = 2 * MXU_WIDTH
NA_GROUP_ROWS = 4
NA_KEY_ROWS = NA_GROUP_ROWS + NA_WIN_ROWS
SLAB_ROWS = 256


def _vmem_params(block_bytes, temp_bytes=0):
    need = 2 * block_bytes + temp_bytes + (4 << 20)
    return pltpu.CompilerParams(vmem_limit_bytes=int(min(max(need, 16 << 20), VMEM_CAP)))


def _tile(n, pref, quantum):
    t = min(pref, n)
    t -= t % quantum
    while t > quantum and n % t:
        t -= quantum
    assert t >= quantum and n % t == 0, (n, pref, quantum)
    return t


def _rope_slab(xs, cos, sin_signed):
    half = MLA_ROPE // 2
    lane = lax.broadcasted_iota(jnp.int32, xs.shape, 1)
    first_half = (lane % MLA_ROPE) < half
    partner = jnp.where(first_half, pltpu.roll(xs, LANE - half, 1), pltpu.roll(xs, half, 1))
    return xs * cos + partner * sin_signed


def _proj_kernel(coef_ref, x_ref, w_ref, b_ref, cos_ref, sin_ref, o_ref, *ot_ref,
                 silu_ranges, sigmoid_range, rope_range, t_block, m_slab):
    j = pl.program_id(1)
    c1 = coef_ref[j]

    def in_range(r):
        return (j >= r[0]) & (j < r[1])

    is_rope = in_range(rope_range)
    is_t = (j == t_block) if t_block is not None else (j < 0)
    is_silu = functools.reduce(jnp.logical_or, [in_range(r) for r in silu_ranges], j < 0)
    is_sigmoid = in_range(sigmoid_range)

    def slabbed(epilogue):
        for m in range(x_ref.shape[0] // m_slab):
            rows = pl.ds(m * m_slab, m_slab)
            acc = _qk_scores(x_ref[rows, :], w_ref[0])
            o_ref[rows, :] = epilogue(acc).astype(o_ref.dtype)

    @pl.when(is_silu)
    def _():
        slabbed(lambda acc: acc * jax.nn.sigmoid(acc))

    @pl.when(is_sigmoid)
    def _():
        bias = b_ref[...]
        slabbed(lambda acc: jax.nn.sigmoid(acc + bias))

    @pl.when(jnp.logical_not(is_silu | is_sigmoid | is_rope | is_t))
    def _():
        slabbed(lambda acc: acc * c1)

    @pl.when(is_rope)
    def _():
        acc = _qk_scores(x_ref[...], w_ref[0])
        cos = cos_ref[...]
        sin = sin_ref[...]
        for s in range(acc.shape[1] // LANE):
            sl = slice(s * LANE, (s + 1) * LANE)
            o_ref[:, sl] = (_rope_slab(acc[:, sl], cos, sin) * c1).astype(o_ref.dtype)

    if t_block is not None:
        @pl.when(is_t)
        def _():
            acc = _qk_scores(x_ref[...], w_ref[0])
            o_ref[...] = acc.astype(o_ref.dtype)
            ot_ref[0][...] = acc.T.astype(ot_ref[0].dtype)


def _project(x_bf, w_t, row0, n, bias, coefs, cos_t, sin_t, seq, layer, tn, *,
             silu_ranges=(), sigmoid_range=(0, 0), rope_range=(0, 0), t_block=None):
    t, k = x_bf.shape
    tm = _tile(math.gcd(t, seq), 1024, 8)
    nj = n // tn
    pos_blocks = seq // tm
    out_specs = [pl.BlockSpec((tm, tn), lambda i, j: (i, j))]
    out_shape = [jax.ShapeDtypeStruct((t, n), jnp.bfloat16)]
    if t_block is not None:
        out_specs.append(pl.BlockSpec((tn, tm), lambda i, j: (0, i)))
        out_shape.append(jax.ShapeDtypeStruct((tn, t), jnp.bfloat16))
    block_bytes = (tm * k // 2 + k * tn + len(out_specs) * tm * tn) * 2 + tn * 4 + 2 * tm * LANE * 4
    return pl.pallas_call(
        functools.partial(_proj_kernel, silu_ranges=silu_ranges, sigmoid_range=sigmoid_range, rope_range=rope_range,
                          t_block=t_block, m_slab=_tile(tm, SLAB_ROWS, 8)),
        grid=(t // tm, nj),
        in_specs=[
            pl.BlockSpec(memory_space=pltpu.SMEM),
            pl.BlockSpec((tm, k), lambda i, j: (i, 0), pipeline_mode=pl.Buffered(1)),
            pl.BlockSpec((pl.Element(1), pl.Element(tn), pl.Element(k)),
                         lambda i, j: (layer, pl.multiple_of(row0 + j * tn, math.gcd(row0, tn)), 0)),
            pl.BlockSpec((None, 1, tn), lambda i, j: (layer, 0, j)),
            pl.BlockSpec((tm, LANE), lambda i, j: (i % pos_blocks, 0)),
            pl.BlockSpec((tm, LANE), lambda i, j: (i % pos_blocks, 0)),
        ],
        out_specs=out_specs,
        out_shape=out_shape,
        compiler_params=_vmem_params(block_bytes, 3 * tm * tn * 4),
        name="project",
    )(coefs, x_bf, w_t, bias, cos_t, sin_t)


def _rms_normed(c, g_ref):
    c = c.astype(jnp.float32)
    ms = jnp.mean(c * c, axis=-1, keepdims=True)
    return (c * lax.rsqrt(ms + RMS_EPS) * g_ref[...]).astype(jnp.bfloat16)


def _mla_q_kernel(h_ref, g_ref, w_ref, cos_ref, sin_ref, o_ref, *, scale):
    cn = _rms_normed(h_ref[:, :MLA_Q_RANK], g_ref)
    acc = jnp.dot(cn, w_ref[...], preferred_element_type=jnp.float32)
    cos = cos_ref[...]
    sin = sin_ref[...]
    for h in range(MLA_HEADS):
        base = h * MLA_QK_PAD
        o_ref[:, base:base + LANE] = (acc[:, base:base + LANE] * scale).astype(o_ref.dtype)
        rope = _rope_slab(acc[:, base + LANE:base + 2 * LANE], cos, sin)
        o_ref[:, base + LANE:base + 2 * LANE] = (rope * scale).astype(o_ref.dtype)


def _mla_q(h1, q_norm, w_uq_pad, cos_t, sin_t, seq, layer):
    t = h1.shape[0]
    tm = _tile(math.gcd(t, seq), 512, 8)
    pos_blocks = seq // tm
    n = MLA_HEADS * MLA_QK_PAD
    assert H1_BCQ % MLA_IN_BLOCK == 0
    block_bytes = (tm * MLA_IN_BLOCK + MLA_Q_RANK * n + tm * n) * 2 + 2 * tm * LANE * 4
    return pl.pallas_call(
        functools.partial(_mla_q_kernel, scale=(MLA_NOPE + MLA_ROPE) ** -0.5 * LOG2E),
        grid=(t // tm,),
        in_specs=[
            pl.BlockSpec((tm, MLA_IN_BLOCK), lambda i: (i, H1_BCQ // MLA_IN_BLOCK)),
            pl.BlockSpec((None, 1, MLA_Q_RANK), lambda i: (layer, 0, 0)),
            pl.BlockSpec((None, MLA_Q_RANK, n), lambda i: (layer, 0, 0)),
            pl.BlockSpec((tm, LANE), lambda i: (i % pos_blocks, 0)),
            pl.BlockSpec((tm, LANE), lambda i: (i % pos_blocks, 0)),
        ],
        out_specs=pl.BlockSpec((tm, n), lambda i: (i, 0)),
        out_shape=jax.ShapeDtypeStruct((t, n), jnp.bfloat16),
        compiler_params=_vmem_params(block_bytes, tm * (2 * n + 3 * MLA_Q_RANK) * 4),
        name="mla_q",
    )(h1, q_norm, w_uq_pad, cos_t, sin_t)


def _mla_kv_kernel(h_ref, g_ref, w_ref, kr_ref, k_ref, vt_ref):
    acc = jnp.dot(_rms_normed(h_ref[...], g_ref), w_ref[...], preferred_element_type=jnp.float32)
    kr = kr_ref[...]
    for h in range(MLA_HEADS):
        base = h * MLA_QK_PAD
        k_ref[:, base:base + LANE] = acc[:, h * MLA_NOPE:(h + 1) * MLA_NOPE].astype(k_ref.dtype)
        k_ref[:, base + LANE:base + 2 * LANE] = kr
    vt_ref[...] = acc[:, MLA_HEADS * MLA_NOPE:].T.astype(vt_ref.dtype)


def _mla_kv(h1, kv_norm, w_ukv_r, k_rope, layer):
    t = h1.shape[0]
    tm = _tile(t, 512, LANE)
    nk = MLA_HEADS * MLA_QK_PAD
    nw = MLA_HEADS * (MLA_NOPE + MLA_V)
    assert H1_BCKV % MLA_KV_RANK == 0
    block_bytes = (tm * MLA_KV_RANK + MLA_KV_RANK * nw + tm * LANE + tm * nk + tm * WIDTH_B) * 2
    return pl.pallas_call(
        _mla_kv_kernel,
        grid=(t // tm,),
        in_specs=[
            pl.BlockSpec((tm, MLA_KV_RANK), lambda i: (i, H1_BCKV // MLA_KV_RANK)),
            pl.BlockSpec((None, 1, MLA_KV_RANK), lambda i: (layer, 0, 0)),
            pl.BlockSpec((None, MLA_KV_RANK, nw), lambda i: (layer, 0, 0)),
            pl.BlockSpec((tm, LANE), lambda i: (i, 0)),
        ],
        out_specs=[
            pl.BlockSpec((tm, nk), lambda i: (i, 0)),
            pl.BlockSpec((WIDTH_B, tm), lambda i: (0, i)),
        ],
        out_shape=[
            jax.ShapeDtypeStruct((t, nk), jnp.bfloat16),
            jax.ShapeDtypeStruct((WIDTH_B, t), jnp.bfloat16),
        ],
        compiler_params=_vmem_params(block_bytes, tm * (2 * nw + 2 * MLA_KV_RANK) * 4),
        name="mla_kv",
    )(h1, kv_norm, w_ukv_r, k_rope)


def _qk_scores(a, b):
    return lax.dot_general(a, b, (((1,), (1,)), ((), ())), preferred_element_type=jnp.float32)


def _softmax_exp2(s):
    p = jnp.exp2(s - jnp.max(s, axis=-1, keepdims=True))
    return p, 1.0 / jnp.sum(p, axis=-1, keepdims=True)


def _pipelined_chains(k_ref, vt_ref, q_fns):
    def scores(q_fn):
        return _qk_scores(k_ref[...], q_fn())

    outs = []
    st_next = scores(q_fns[0])
    for u in range(len(q_fns)):
        st = st_next
        if u + 1 < len(q_fns):
            st_next = scores(q_fns[u + 1])
        e = jnp.exp2(st - jnp.max(st, axis=0, keepdims=True))
        rl = 1.0 / jnp.sum(e, axis=0, keepdims=True)
        outs.append(jnp.dot(vt_ref[...], e.astype(jnp.bfloat16), preferred_element_type=jnp.float32) * rl)
    return outs


def _attn_b_kernel(q_ref, k_ref, vt_ref, g_ref, o_ref):
    n_chain = q_ref.shape[0] // ATTN_CHAIN
    row_sl = [pl.ds(c * ATTN_CHAIN, ATTN_CHAIN) for c in range(n_chain)]
    outs = _pipelined_chains(k_ref, vt_ref, [functools.partial(lambda rows: q_ref[rows, :], rows) for rows in row_sl])
    for rows, ot in zip(row_sl, outs):
        o_ref[rows, :] = (ot.T * g_ref[rows, :].astype(jnp.float32)).astype(o_ref.dtype)


def _attn_b(q_pad, k_pad, vt, h2, batch, seq):
    t = q_pad.shape[0]
    tq = _tile(seq, 1024, ATTN_CHAIN)
    nq = seq // tq
    block_bytes = (tq * MLA_QK_PAD + seq * MLA_QK_PAD + seq * MLA_V + 2 * tq * MLA_V) * 2
    return pl.pallas_call(
        _attn_b_kernel,
        grid=(batch, MLA_HEADS, nq),
        in_specs=[
            pl.BlockSpec((tq, MLA_QK_PAD), lambda b, hd, qi: (b * nq + qi, hd)),
            pl.BlockSpec((seq, MLA_QK_PAD), lambda b, hd, qi: (b, hd)),
            pl.BlockSpec((MLA_V, seq), lambda b, hd, qi: (hd, b)),
            pl.BlockSpec((tq, MLA_V), lambda b, hd, qi: (b * nq + qi, H2_BG // MLA_V + hd)),
        ],
        out_specs=pl.BlockSpec((tq, MLA_V), lambda b, hd, qi: (b * nq + qi, hd)),
        out_shape=jax.ShapeDtypeStruct((t, WIDTH_B), jnp.bfloat16),
        compiler_params=_vmem_params(block_bytes, 3 * tq * seq * 4),
        name="attn_b",
    )(q_pad, k_pad, vt, h2)


def _attn_c_kernel(lq1_ref, lk1_ref, lq2_ref, lk2_ref, q_ref, k_ref, vt_ref, g_ref, sub_ref, o_ref, *, lam_init):
    lam = (jnp.exp(jnp.sum(lq1_ref[...] * lk1_ref[...], keepdims=True))
           - jnp.exp(jnp.sum(lq2_ref[...] * lk2_ref[...], keepdims=True)) + lam_init)
    n_chain = q_ref.shape[0] // ATTN_CHAIN
    row_sl = [pl.ds(c * ATTN_CHAIN, ATTN_CHAIN) for c in range(n_chain)]

    def masked_q(rows, second_map):
        q = q_ref[rows, :]
        lane = lax.broadcasted_iota(jnp.int32, q.shape, 1)
        keep = (lane >= DIFF_QK) if second_map else (lane < DIFF_QK)
        return jnp.where(keep, q, jnp.zeros_like(q))

    outs = _pipelined_chains(k_ref, vt_ref, [functools.partial(masked_q, rows, second)
                                             for rows in row_sl for second in (False, True)])
    for c, rows in enumerate(row_sl):
        ot = outs[2 * c] - lam * outs[2 * c + 1]
        ms = jnp.mean(ot * ot, axis=0, keepdims=True)
        o = (ot * lax.rsqrt(ms + RMS_EPS)).T * sub_ref[...] * (1.0 - lam_init)
        o_ref[rows, :] = (o * g_ref[rows, :].astype(jnp.float32)).astype(o_ref.dtype)


def _attn_c(h2, vt, lam_q1, lam_k1, lam_q2, lam_k2, diff_subln, layer, batch, seq):
    t = h2.shape[0]
    tq = _tile(seq, 1024, ATTN_CHAIN)
    nq = seq // tq
    lam_init = 0.8 - 0.6 * math.exp(-0.3 * layer)
    lam_spec = pl.BlockSpec((None, 1, DIFF_QK), lambda b, hd, qi: (layer, 0, 0))
    block_bytes = (3 * tq * DIFF_V + 2 * seq * DIFF_V) * 2
    return pl.pallas_call(
        functools.partial(_attn_c_kernel, lam_init=lam_init),
        grid=(batch, DIFF_HEADS, nq),
        in_specs=[
            lam_spec, lam_spec, lam_spec, lam_spec,
            pl.BlockSpec((tq, DIFF_V), lambda b, hd, qi: (b * nq + qi, H2_CQ // DIFF_V + hd)),
            pl.BlockSpec((seq, DIFF_V), lambda b, hd, qi: (b, H2_CK // DIFF_V + hd)),
            pl.BlockSpec((DIFF_V, seq), lambda b, hd, qi: (hd, b)),
            pl.BlockSpec((tq, DIFF_V), lambda b, hd, qi: (b * nq + qi, H2_CG // DIFF_V + hd)),
            pl.BlockSpec((None, 1, DIFF_V), lambda b, hd, qi: (layer, 0, 0)),
        ],
        out_specs=pl.BlockSpec((tq, DIFF_V), lambda b, hd, qi: (b * nq + qi, hd)),
        out_shape=jax.ShapeDtypeStruct((t, WIDTH_C), jnp.bfloat16),
        compiler_params=_vmem_params(block_bytes, 6 * tq * seq * 4),
        name="attn_c",
    )(lam_q1, lam_k1, lam_q2, lam_k2, h2, h2, vt, h2, diff_subln)


def _na_bias_table(rpb, rows):
    n_groups = rows // NA_GROUP_ROWS
    kh = NA_WIN_ROWS
    a = np.arange(NA_GROUP_ROWS)[:, None]
    i = np.arange(NA_KEY_ROWS)[None, :]
    dr_list, row_ok_list = [], []
    for grp in (0, 1, n_groups - 1):
        k_start = NA_GROUP_ROWS * min(max(grp - 1, 0), n_groups - 3)
        r = NA_GROUP_ROWS * grp + a
        row_start = np.clip(r - kh // 2, 0, rows - kh)
        k_row = k_start + i
        row_ok_list.append((k_row >= row_start) & (k_row < row_start + kh))
        dr_list.append(np.clip(k_row - r + (NA_WIN_ROWS - 1), 0, 2 * NA_WIN_ROWS - 2))
    dr = np.stack(dr_list)
    row_ok = np.stack(row_ok_list)
    c = np.arange(GRID_W)
    col_start = np.clip(c - NA_WIN_COLS // 2, 0, GRID_W - NA_WIN_COLS)
    col_ok = (c[None, :] >= col_start[:, None]) & (c[None, :] < col_start[:, None] + NA_WIN_COLS)
    dc = np.clip(c[None, :] - c[:, None] + (NA_WIN_COLS - 1), 0, 2 * NA_WIN_COLS - 2)
    ok = row_ok[:, None, :, None, :, None] & col_ok[None, None, None, :, None, :]
    pick_row = (dr[..., None] == np.arange(2 * NA_WIN_ROWS - 1)).astype(np.float32)
    pick_col = (dc[..., None] == np.arange(2 * NA_WIN_COLS - 1)).astype(np.float32)
    bias = jnp.einsum('vair,lhrc,qkc->lvhaqik', pick_row, rpb.astype(jnp.float32) * LOG2E, pick_col,
                      precision=lax.Precision.HIGHEST)
    bias = jnp.where(ok[None], bias, NEG_BIG)
    return bias.reshape(rpb.shape[0], 3, NA_HEADS, NA_GROUP_ROWS * GRID_W, NA_KEY_ROWS * GRID_W)


def _attn_a_kernel(q_ref, k_ref, v_ref, g_ref, bias_ref, o_ref, *, n_groups):
    j = pl.program_id(2)
    gq = q_ref.shape[0]
    gk = bias_ref.shape[-1]
    start = pl.multiple_of(jnp.clip(j - 1, 0, n_groups - 3) * gq, gq)
    p, rl = _softmax_exp2(_qk_scores(q_ref[...], k_ref[pl.ds(start, gk), :]) + bias_ref[...])
    o = jnp.dot(p.astype(jnp.bfloat16), v_ref[pl.ds(start, gk), :], preferred_element_type=jnp.float32)
    o_ref[...] = (o * rl * g_ref[...].astype(jnp.float32)).astype(o_ref.dtype)


def _attn_a(h1, bias, layer, batch, seq):
    t = h1.shape[0]
    gq = NA_GROUP_ROWS * GRID_W
    gk = NA_KEY_ROWS * GRID_W
    n_groups = seq // gq
    assert n_groups >= 3

    def variant(j):
        return jnp.where(j == 0, 0, jnp.where(j == n_groups - 1, 2, 1))

    block_bytes = (3 * gq * HEAD_DIM + 2 * seq * HEAD_DIM) * 2 + gq * gk * 4
    return pl.pallas_call(
        functools.partial(_attn_a_kernel, n_groups=n_groups),
        grid=(batch, NA_HEADS, n_groups),
        in_specs=[
            pl.BlockSpec((gq, HEAD_DIM), lambda b, hd, j: (b * n_groups + j, H1_AQ // HEAD_DIM + hd)),
            pl.BlockSpec((seq, HEAD_DIM), lambda b, hd, j: (b, H1_AK // HEAD_DIM + hd)),
            pl.BlockSpec((seq, HEAD_DIM), lambda b, hd, j: (b, H1_AV // HEAD_DIM + hd)),
            pl.BlockSpec((gq, HEAD_DIM), lambda b, hd, j: (b * n_groups + j, H1_AG // HEAD_DIM + hd)),
            pl.BlockSpec((None, None, None, gq, gk), lambda b, hd, j: (layer, variant(j), hd, 0, 0)),
        ],
        out_specs=pl.BlockSpec((gq, HEAD_DIM), lambda b, hd, j: (b * n_groups + j, hd)),
        out_shape=jax.ShapeDtypeStruct((t, WIDTH_A), jnp.bfloat16),
        compiler_params=_vmem_params(block_bytes, 4 * gq * gk * 4),
        name="attn_a",
    )(h1, h1, h1, h1, bias)


def _merge_kernel(ya_ref, yb_ref, yc_ref, wa_ref, wb_ref, wc_ref, ga_ref, gb_ref, gc_ref, o_ref, *, m_slab):
    for m in range(o_ref.shape[0] // m_slab):
        rows = pl.ds(m * m_slab, m_slab)

        def branch(y_ref, w_ref, g_ref):
            y = jnp.dot(y_ref[rows, :], w_ref[...], preferred_element_type=jnp.float32)
            return g_ref[rows, :].astype(jnp.float32) * y

        merged = branch(ya_ref, wa_ref, ga_ref) + branch(yb_ref, wb_ref, gb_ref) + branch(yc_ref, wc_ref, gc_ref)
        o_ref[rows, :] = merged.astype(o_ref.dtype)


def _merge(ya, yb, yc, w_o_a, w_o_b, w_o_c, h2, d_model, layer):
    t = ya.shape[0]
    tm = _tile(t, 512, 8)
    tn = _tile(d_model, 1024, LANE)
    y_spec = pl.BlockSpec((tm, WIDTH_A), lambda i, j: (i, 0))
    w_spec = pl.BlockSpec((None, WIDTH_A, tn), lambda i, j: (layer, 0, j))

    def gate_spec(branch):
        off = (H2_MERGE + branch * d_model) // tn
        return pl.BlockSpec((tm, tn), lambda i, j: (i, off + j))

    assert H2_MERGE % tn == 0 and d_model % tn == 0
    block_bytes = (3 * tm * WIDTH_A + 3 * WIDTH_A * tn + 4 * tm * tn) * 2
    return pl.pallas_call(
        functools.partial(_merge_kernel, m_slab=_tile(tm, SLAB_ROWS, 8)),
        grid=(t // tm, d_model // tn),
        in_specs=[y_spec, y_spec, y_spec, w_spec, w_spec, w_spec, gate_spec(0), gate_spec(1), gate_spec(2)],
        out_specs=pl.BlockSpec((tm, tn), lambda i, j: (i, j)),
        out_shape=jax.ShapeDtypeStruct((t, d_model), jnp.bfloat16),
        compiler_params=_vmem_params(block_bytes, 4 * tm * tn * 4),
        name="merge",
    )(ya, yb, yc, w_o_a, w_o_b, w_o_c, h2, h2, h2)


def _out_kernel(m_ref, w_ref, x_ref, o_ref, *, alpha, m_slab):
    for m in range(o_ref.shape[0] // m_slab):
        rows = pl.ds(m * m_slab, m_slab)
        y = jnp.dot(m_ref[rows, :], w_ref[...], preferred_element_type=jnp.float32)
        o_ref[rows, :] = alpha * x_ref[rows, :] + y


def _out_proj(merged, w_out, x, alpha, layer):
    t, d = x.shape
    tm = _tile(t, 512, 8)
    tn = _tile(d, 1024, LANE)
    block_bytes = (tm * d + d * tn) * 2 + 2 * tm * tn * 4
    return pl.pallas_call(
        functools.partial(_out_kernel, alpha=alpha, m_slab=_tile(tm, SLAB_ROWS, 8)),
        grid=(t // tm, d // tn),
        in_specs=[
            pl.BlockSpec((tm, d), lambda i, j: (i, 0)),
            pl.BlockSpec((None, d, tn), lambda i, j: (layer, 0, j)),
            pl.BlockSpec((tm, tn), lambda i, j: (i, j)),
        ],
        out_specs=pl.BlockSpec((tm, tn), lambda i, j: (i, j)),
        out_shape=jax.ShapeDtypeStruct((t, d), jnp.float32),
        compiler_params=_vmem_params(block_bytes, 2 * tm * tn * 4),
        name="out_proj",
    )(merged, w_out, x)


def _ln_kernel(z_ref, g_ref, b_ref, o_ref, obf_ref):
    z = z_ref[...]
    mu = jnp.mean(z, axis=-1, keepdims=True)
    zc = z - mu
    var = jnp.mean(zc * zc, axis=-1, keepdims=True)
    out = zc * lax.rsqrt(var + LN_EPS) * g_ref[...] + b_ref[...]
    o_ref[...] = out
    obf_ref[...] = out.astype(obf_ref.dtype)


def _layer_norm(z, ln_g, ln_b, layer):
    t, d = z.shape
    tm = _tile(t, 256, 8)
    row = pl.BlockSpec((tm, d), lambda i: (i, 0))
    vec = pl.BlockSpec((None, 1, d), lambda i: (layer, 0, 0))
    return pl.pallas_call(
        _ln_kernel,
        grid=(t // tm,),
        in_specs=[row, vec, vec],
        out_specs=[row, row],
        out_shape=[jax.ShapeDtypeStruct((t, d), jnp.float32), jax.ShapeDtypeStruct((t, d), jnp.bfloat16)],
        compiler_params=_vmem_params(tm * d * 10, 3 * tm * d * 4),
        name="layer_norm",
    )(z, ln_g, ln_b)


def _rope_tables(seq):
    half = MLA_ROPE // 2
    inv_freq = ROPE_THETA ** (-jnp.arange(half, dtype=jnp.float32) * 2.0 / MLA_ROPE)
    ang = jnp.arange(seq, dtype=jnp.float32)[:, None] * inv_freq[None, :]
    cos = jnp.cos(ang)
    sin = jnp.sin(ang)
    cos_t = jnp.concatenate([cos, cos, cos, cos], axis=-1)
    sin_t = jnp.concatenate([-sin, sin, -sin, sin], axis=-1)
    return cos_t, sin_t


def _row3(v):
    return v.astype(jnp.float32)[:, None, :]


def kernel(x, w_in, w_uq, q_norm, w_ukv, kv_norm, na_rpb, lam_q1, lam_k1, lam_q2, lam_k2, diff_subln,
           w_o_a, w_o_b, w_o_c, b_merge, w_out, ln_g, ln_b):
    batch, seq, d_model = x.shape
    depth = w_in.shape[0]
    t = batch * seq
    f32, bf16 = jnp.float32, jnp.bfloat16
    tn = PROJ_TN
    assert seq % (NA_GROUP_ROWS * GRID_W) == 0 and seq // GRID_W >= NA_WIN_ROWS + NA_GROUP_ROWS
    assert H1_WIDTH % tn == 0 and H2_MERGE % tn == 0 and d_model % tn == 0 and WIDTH_C == tn
    blk = lambda off: off // tn

    kr_lo, kr_hi = H1_WIDTH, H1_WIDTH + MLA_ROPE
    w_in_t = jnp.swapaxes(w_in, 1, 2).astype(bf16)
    n1, n2 = H1_WIDTH, w_in.shape[2] - kr_hi
    w_uq_pad = jnp.pad(w_uq.reshape(depth, MLA_Q_RANK, MLA_HEADS, MLA_NOPE + MLA_ROPE),
                       ((0, 0), (0, 0), (0, 0), (0, MLA_QK_PAD - MLA_NOPE - MLA_ROPE)))
    w_uq_pad = w_uq_pad.reshape(depth, MLA_Q_RANK, MLA_HEADS * MLA_QK_PAD).astype(bf16)
    w_ukv_h = w_ukv.reshape(depth, MLA_KV_RANK, MLA_HEADS, MLA_NOPE + MLA_V)
    w_ukv_r = jnp.concatenate([w_ukv_h[..., :MLA_NOPE].reshape(depth, MLA_KV_RANK, -1),
                               w_ukv_h[..., MLA_NOPE:].reshape(depth, MLA_KV_RANK, -1)], axis=-1).astype(bf16)
    w_oa, w_ob, w_oc, w_o = (w.astype(bf16) for w in (w_o_a, w_o_b, w_o_c, w_out))
    bias1 = jnp.zeros((depth, 1, n1), f32)
    bias2 = jnp.concatenate([jnp.zeros((depth, H2_MERGE), f32), b_merge.astype(f32)], axis=1)[:, None, :]
    bias_kr = jnp.zeros((depth, 1, LANE), f32)
    coef1 = np.ones((n1 // tn,), np.float32)
    coef1[blk(H1_AQ)] = HEAD_DIM ** -0.5 * LOG2E
    coef2 = np.ones((n2 // tn,), np.float32)
    coef2[blk(H2_CQ)] = DIFF_QK ** -0.5 * LOG2E
    na_bias = _na_bias_table(na_rpb, seq // GRID_W)
    cos_t, sin_t = _rope_tables(seq)
    q_norm3, kv_norm3, subln3, ln_g3, ln_b3 = (_row3(v) for v in (q_norm, kv_norm, diff_subln, ln_g, ln_b))
    lams = [_row3(v) for v in (lam_q1, lam_k1, lam_q2, lam_k2)]
    alpha = (2.0 * depth) ** 0.25

    xf = x.reshape(t, d_model)
    x_bf = xf.astype(bf16)
    for l in range(depth):
        h1, = _project(x_bf, w_in_t, 0, n1, bias1, jnp.asarray(coef1), cos_t, sin_t, seq, l, tn,
                       silu_ranges=((blk(H1_AG), blk(H1_BCQ)),))
        h2, vt_c = _project(x_bf, w_in_t, kr_hi, n2, bias2, jnp.asarray(coef2), cos_t, sin_t, seq, l, tn,
                            silu_ranges=((blk(H2_BG), blk(H2_CQ)), (blk(H2_CG), blk(H2_MERGE))),
                            sigmoid_range=(blk(H2_MERGE), n2 // tn), rope_range=(blk(H2_CQ), blk(H2_CV)),
                            t_block=blk(H2_CV))
        k_rope, = _project(x_bf, w_in_t, kr_lo, LANE, bias_kr, jnp.ones((1,), f32), cos_t, sin_t, seq, l, LANE,
                           rope_range=(0, 1))

        q_pad = _mla_q(h1, q_norm3, w_uq_pad, cos_t, sin_t, seq, l)
        k_pad, vt_b = _mla_kv(h1, kv_norm3, w_ukv_r, k_rope, l)
        yb = _attn_b(q_pad, k_pad, vt_b, h2, batch, seq)
        yc = _attn_c(h2, vt_c, *lams, subln3, l, batch, seq)
        ya = _attn_a(h1, na_bias, l, batch, seq)

        merged = _merge(ya, yb, yc, w_oa, w_ob, w_oc, h2, d_model, l)
        z = _out_proj(merged, w_o, xf, alpha, l)
        xf, x_bf = _layer_norm(z, ln_g3, ln_b3, l)
    return xf.reshape(batch, seq, d_model)
```
